```python
import math
import jax
import jax.numpy as jnp
from jax import lax
import numpy as np


D_MODEL = 2048
BATCH = 1
SEQ = 8192
DEPTH = 2

GRID_W = 64
CTX_LEN = 256
EPS = 1e-6
N_MOD = 6
S5_WIDTH = 1024
S5_GROUP_CH = 16
S5_GROUPS = S5_WIDTH // S5_GROUP_CH
S5_STATE = 64
S5_DT_MIN = 1e-3
S5_DT_MAX = 1e-1
GLA_HEADS = 4
GLA_DK = 128
GLA_DV = 256
GLA_KW = GLA_HEADS * GLA_DK
GLA_VW = GLA_HEADS * GLA_DV
GLA_RANK = 16
GLA_TAU = 16.0
GLA_CHUNK = 64
ATT_HEADS = 16
ATT_KV_HEADS = 4
ATT_HEAD_DIM = 64
ATT_QW = ATT_HEADS * ATT_HEAD_DIM
ATT_KVW = ATT_KV_HEADS * ATT_HEAD_DIM
WINDOW = 128
ATT_BLOCK = 128
ROPE_BASE = 10000.0
N_BRANCH = 3
IN_SPLITS = (S5_WIDTH, GLA_KW, GLA_KW, GLA_VW, GLA_VW, ATT_QW, ATT_KVW, ATT_KVW, N_BRANCH * D_MODEL)
IN_WIDTH = S5_WIDTH + 2 * GLA_KW + 2 * GLA_VW + ATT_QW + 2 * ATT_KVW + N_BRANCH * D_MODEL
N_EXPERTS = 16
EXPERT_FF = 2048
EC_CAPACITY = 2

kernel_name = 'hybrid_s5_gla_swa_ec_diffusion_block'


def rms_norm(x, g):
    x32 = x.astype(jnp.float32)
    r = lax.rsqrt(jnp.mean(x32 * x32, axis=-1, keepdims=True) + EPS)
    return (x32 * r).astype(x.dtype) * g


def modulate(h, shift, scale):
    return h * (1 + scale) + shift


def split_heads(t, n_heads):
    return t.reshape(t.shape[:-1] + (n_heads, t.shape[-1] // n_heads))


def flip_seq(t):
    return jnp.flip(t, axis=1)


def keep_seq(t):
    return t


def split_projection(z):
    offsets = []
    acc = 0
    for w in IN_SPLITS[:-1]:
        acc += w
        offsets.append(acc)
    return jnp.split(z, offsets, axis=-1)


def axial_rope_tables(n_rows, dtype):
    row = jnp.repeat(jnp.arange(n_rows), GRID_W).astype(jnp.float32)
    col = jnp.tile(jnp.arange(GRID_W), n_rows).astype(jnp.float32)
    n_freq = ATT_HEAD_DIM // 4
    inv_freq = ROPE_BASE ** (-jnp.arange(n_freq, dtype=jnp.float32) / n_freq)
    ang = jnp.concatenate([row[:, None] * inv_freq, col[:, None] * inv_freq], axis=-1)
    return jnp.cos(ang).astype(dtype), jnp.sin(ang).astype(dtype)


def apply_rope(t, cos, sin):
    half = t.shape[-1] // 2
    t1, t2 = t[..., :half], t[..., half:]
    cs, sn = cos[None, :, None, :], sin[None, :, None, :]
    return jnp.concatenate([t1 * cs - t2 * sn, t1 * sn + t2 * cs], axis=-1)


def s5_discretize(lam_re, lam_im, log_dt, b_re, b_im):
    dt = jnp.exp(log_dt)[:, None]
    mag = jnp.exp(lam_re * dt)
    ab_re = mag * jnp.cos(lam_im * dt)
    ab_im = mag * jnp.sin(lam_im * dt)
    den = lam_re * lam_re + lam_im * lam_im
    nr = ab_re - 1
    f_re = ((nr * lam_re + ab_im * lam_im) / den)[..., None]
    f_im = ((ab_im * lam_re - nr * lam_im) / den)[..., None]
    bb_re = f_re * b_re - f_im * b_im
    bb_im = f_re * b_im + f_im * b_re
    return ab_re, ab_im, bb_re, bb_im


def complex_affine_combine(e1, e2):
    a1r, a1i, b1r, b1i = e1
    a2r, a2i, b2r, b2i = e2
    return (a2r * a1r - a2i * a1i, a2r * a1i + a2i * a1r,
            a2r * b1r - a2i * b1i + b2r, a2r * b1i + a2i * b1r + b2i)


def s5_states(u, h0_re, h0_im, disc):
    ab_re, ab_im, bb_re, bb_im = disc
    bu_re = jnp.einsum('blgc,gnc->blgn', u, bb_re)
    bu_im = jnp.einsum('blgc,gnc->blgn', u, bb_im)
    bu_re = bu_re.at[:, 0].add(ab_re * h0_re - ab_im * h0_im)
    bu_im = bu_im.at[:, 0].add(ab_re * h0_im + ab_im * h0_re)
    a_re = jnp.broadcast_to(ab_re, bu_re.shape)
    a_im = jnp.broadcast_to(ab_im, bu_im.shape)
    _, _, h_re, h_im = lax.associative_scan(complex_affine_combine, (a_re, a_im, bu_re, bu_im), axis=1)
    return h_re, h_im


def s5_readout(h_re, h_im, c_re, c_im):
    return jnp.einsum('blgn,gcn->blgc', h_re, c_re) - jnp.einsum('blgn,gcn->blgc', h_im, c_im)


def s5_glu(y, w_glu):
    g = jax.nn.gelu(y)
    return g * jax.nn.sigmoid(g @ w_glu)


def s5_branch(u, uc, lp, with_ctx_out):
    B, L, _ = u.shape
    Lc = uc.shape[1]
    ug = u.reshape(B, L, S5_GROUPS, S5_GROUP_CH)
    ucg = uc.reshape(B, Lc, S5_GROUPS, S5_GROUP_CH)
    zero = jnp.zeros((B, S5_GROUPS, S5_STATE), u.dtype)
    y = u * lp['s5_d']
    yc = uc * lp['s5_d'] if with_ctx_out else None
    for d in range(2):
        rev = flip_seq if d == 1 else keep_seq
        disc = s5_discretize(lp['s5_lam_re'][d], lp['s5_lam_im'][d], lp['s5_log_dt'][d],
                             lp['s5_b_re'][d], lp['s5_b_im'][d])
        c_re, c_im = lp['s5_c_re'][d], lp['s5_c_im'][d]
        hc_re, hc_im = s5_states(rev(ucg), zero, zero, disc)
        if with_ctx_out:
            yc = yc + rev(s5_readout(hc_re, hc_im, c_re, c_im)).reshape(B, Lc, S5_WIDTH)
        h_re, h_im = s5_states(rev(ug), hc_re[:, -1], hc_im[:, -1], disc)
        y = y + rev(s5_readout(h_re, h_im, c_re, c_im)).reshape(B, L, S5_WIDTH)
    y = s5_glu(y, lp['s5_w_glu'])
    yc = s5_glu(yc, lp['s5_w_glu']) if with_ctx_out else None
    return y, yc


def gla_log_decay(h, w1, w2, b):
    return jax.nn.log_sigmoid((h @ w1) @ w2 + b) / GLA_TAU


def gla_direction(q, k, v, log_a, s0):
    B, L, H, dk = q.shape
    n = L // GLA_CHUNK

    def chunked(t):
        return jnp.moveaxis(t.reshape(B, n, GLA_CHUNK, H, t.shape[-1]), 1, 0)

    cum = jnp.cumsum(chunked(log_a), axis=2)
    lower = jnp.tril(jnp.ones((GLA_CHUNK, GLA_CHUNK), dtype=bool))[None, :, :, None, None]

    def step(s, inp):
        qc, kc, vc, bc = inp
        o_inter = jnp.einsum('bihd,bhdv->bihv', qc * jnp.exp(bc), s)
        rel = jnp.exp(jnp.where(lower, bc[:, :, None] - bc[:, None, :], -jnp.inf))
        scores = jnp.einsum('bihd,bjhd,bijhd->bhij', qc, kc, rel)
        o_intra = jnp.einsum('bhij,bjhv->bihv', scores, vc)
        b_last = bc[:, -1]
        s_new = (jnp.exp(b_last)[..., None] * s
                 + jnp.einsum('bjhd,bjhv->bhdv', kc * jnp.exp(b_last[:, None] - bc), vc))
        return s_new, o_inter + o_intra

    s_fin, o = lax.scan(step, s0, (chunked(q), chunked(k), chunked(v), cum))
    return jnp.moveaxis(o, 0, 1).reshape(B, L, H, v.shape[-1]), s_fin


def gla_final_state(k, v, log_a):
    cum = jnp.cumsum(log_a, axis=1)
    b_last = cum[:, -1]
    return jnp.einsum('blhd,blhv->bhdv', k * jnp.exp(b_last[:, None] - cum), v)


def gla_branch(zl, zc, h, hc, lp, with_ctx_out):
    q, k, v, r = [split_heads(t, GLA_HEADS) for t in zl]
    qc, kc, vc, rc = [split_heads(t, GLA_HEADS) for t in zc]
    B, L = q.shape[:2]
    Lc = qc.shape[1]
    scale = GLA_DK ** -0.5
    s0 = jnp.zeros((B, GLA_HEADS, GLA_DK, GLA_DV), v.dtype)
    o = jnp.zeros_like(v)
    oc = jnp.zeros_like(vc) if with_ctx_out else None
    for d in range(2):
        rev = flip_seq if d == 1 else keep_seq
        la = split_heads(gla_log_decay(h, lp['gla_w1'][d], lp['gla_w2'][d], lp['gla_b'][d]), GLA_HEADS)
        lac = split_heads(gla_log_decay(hc, lp['gla_w1'][d], lp['gla_w2'][d], lp['gla_b'][d]), GLA_HEADS)
        if with_ctx_out:
            oc_d, sc = gla_direction(rev(qc) * scale, rev(kc), rev(vc), rev(lac), s0)
            oc = oc + rev(oc_d)
        else:
            sc = gla_final_state(rev(kc), rev(vc), rev(lac))
        o_d, _ = gla_direction(rev(q) * scale, rev(k), rev(v), rev(la), sc)
        o = o + rev(o_d)
    y = (rms_norm(o, lp['gla_norm_g']) * jax.nn.silu(r)).reshape(B, L, GLA_VW)
    yc = (rms_norm(oc, lp['gla_norm_g']) * jax.nn.silu(rc)).reshape(B, Lc, GLA_VW) if with_ctx_out else None
    return y, yc


def window_attention(q, k, v, kc, vc, sink):
    B, L = q.shape[:2]
    Lc = kc.shape[1]
    grp = ATT_HEADS // ATT_KV_HEADS
    nb = L // ATT_BLOCK
    nj = 3 * ATT_BLOCK
    qb = q.reshape(B, nb, ATT_BLOCK, ATT_KV_HEADS, grp, ATT_HEAD_DIM) * ATT_HEAD_DIM ** -0.5
    pad = ((0, 0), (ATT_BLOCK, ATT_BLOCK), (0, 0), (0, 0))

    def band(t):
        tb = jnp.pad(t, pad).reshape(B, nb + 2, ATT_BLOCK, ATT_KV_HEADS, ATT_HEAD_DIM)
        return jnp.concatenate([tb[:, :-2], tb[:, 1:-1], tb[:, 2:]], axis=2)

    kb, vb = band(k), band(v)
    qpos = jnp.arange(L).reshape(nb, ATT_BLOCK, 1)
    kpos = ((jnp.arange(nb) - 1) * ATT_BLOCK)[:, None, None] + jnp.arange(nj)[None, None, :]
    valid = (jnp.abs(kpos - qpos) <= WINDOW) & (kpos >= 0) & (kpos < L)
    s_lat = jnp.einsum('bnqkgd,bnjkd->bnkgqj', qb, kb).astype(jnp.float32)
    s_lat = jnp.where(valid[None, :, None, None], s_lat, -jnp.inf)
    s_ctx = jnp.einsum('bnqkgd,bckd->bnkgqc', qb, kc).astype(jnp.float32)
    s_sink = jnp.broadcast_to(sink.astype(jnp.float32).reshape(1, 1, ATT_KV_HEADS, grp, 1, 1),
                              s_lat.shape[:-1] + (1,))
    p = jax.nn.softmax(jnp.concatenate([s_lat, s_ctx, s_sink], axis=-1), axis=-1).astype(v.dtype)
    o = (jnp.einsum('bnkgqj,bnjkd->bnqkgd', p[..., :nj], vb)
         + jnp.einsum('bnkgqc,bckd->bnqkgd', p[..., nj:nj + Lc], vc))
    return o.reshape(B, L, ATT_QW)


def context_attention(qc, kc, vc, sink):
    B, Lc = qc.shape[:2]
    grp = ATT_HEADS // ATT_KV_HEADS
    qg = qc.reshape(B, Lc, ATT_KV_HEADS, grp, ATT_HEAD_DIM) * ATT_HEAD_DIM ** -0.5
    s = jnp.einsum('bqkgd,bckd->bkgqc', qg, kc).astype(jnp.float32)
    s_sink = jnp.broadcast_to(sink.astype(jnp.float32).reshape(1, ATT_KV_HEADS, grp, 1, 1), s.shape[:-1] + (1,))
    p = jax.nn.softmax(jnp.concatenate([s, s_sink], axis=-1), axis=-1).astype(vc.dtype)
    o = jnp.einsum('bkgqc,bckd->bqkgd', p[..., :Lc], vc)
    return o.reshape(B, Lc, ATT_QW)


def merge_branches(gates, y_s5, y_gla, y_att, lp):
    g = jax.nn.sigmoid(split_heads(gates, N_BRANCH))
    m = (g[..., 0, :] * (y_s5 @ lp['w_branch_s5'])
         + g[..., 1, :] * (y_gla @ lp['w_branch_gla'])
         + g[..., 2, :] * (y_att @ lp['w_branch_attn']))
    return m @ lp['w_out']


def token_mixer(h, hc, lp, cos, sin, with_ctx_out):
    u, qg, kg, vg, rg, qa, ka, va, gates = split_projection(h @ lp['w_in'])
    uc, qgc, kgc, vgc, rgc, qac, kac, vac, gatesc = split_projection(hc @ lp['w_in'])
    y_s5, yc_s5 = s5_branch(u, uc, lp, with_ctx_out)
    y_gla, yc_gla = gla_branch((qg, kg, vg, rg), (qgc, kgc, vgc, rgc), h, hc, lp, with_ctx_out)
    kc_h = split_heads(kac, ATT_KV_HEADS)
    vc_h = split_heads(vac, ATT_KV_HEADS)
    q = apply_rope(split_heads(qa, ATT_HEADS), cos, sin)
    k = apply_rope(split_heads(ka, ATT_KV_HEADS), cos, sin)
    y_att = window_attention(q, k, split_heads(va, ATT_KV_HEADS), kc_h, vc_h, lp['attn_sink'])
    y = merge_branches(gates, y_s5, y_gla, y_att, lp)
    if not with_ctx_out:
        return y, None
    yc_att = context_attention(split_heads(qac, ATT_HEADS), kc_h, vc_h, lp['attn_sink'])
    return y, merge_branches(gatesc, yc_s5, yc_gla, yc_att, lp)


def expert_choice_ffn(h, router, w_gate, w_up, w_down):
    B, N, _ = h.shape
    cap = EC_CAPACITY * N // N_EXPERTS
    aff = jax.nn.softmax((h @ router).astype(jnp.float32), axis=-1)
    gate, idx = lax.top_k(jnp.swapaxes(aff, 1, 2), cap)
    bidx = jnp.arange(B)[:, None, None]
    xs = h[bidx, idx]
    a = jnp.einsum('becd,edf->becf', xs, w_gate)
    up = jnp.einsum('becd,edf->becf', xs, w_up)
    ye = jnp.einsum('becf,efd->becd', jax.nn.silu(a) * up, w_down) * gate[..., None].astype(h.dtype)
    return jnp.zeros_like(h).at[bidx, idx].add(ye)


def setup_inputs(seed: int = 0) -> dict:
    key = jax.random.key(seed)
    ks = iter(jax.random.split(key, 40))
    f32 = jnp.float32

    def nrm(shape, scale):
        return jax.random.normal(next(ks), shape, f32) * scale

    s5_lam_shape = (DEPTH, 2, S5_GROUPS, S5_STATE)
    n_idx = jnp.arange(S5_STATE, dtype=f32)
    return {
        'x': nrm((BATCH, SEQ, D_MODEL), 1.0),
        'c': nrm((BATCH, D_MODEL), 1.0),
        'ctx': nrm((BATCH, CTX_LEN, D_MODEL), 1.0),
        'c_ctx': nrm((D_MODEL,), 1.0),
        'ada_w': nrm((DEPTH, D_MODEL, N_MOD * D_MODEL), 0.5 * D_MODEL ** -0.5),
        'ada_b': nrm((DEPTH, N_MOD * D_MODEL), 0.02),
        'norm1_g': 1.0 + nrm((DEPTH, D_MODEL), 0.02),
        'norm2_g': 1.0 + nrm((DEPTH, D_MODEL), 0.02),
        'w_in': nrm((DEPTH, D_MODEL, IN_WIDTH), D_MODEL ** -0.5),
        's5_lam_re': -0.5 + nrm(s5_lam_shape, 0.01),
        's5_lam_im': math.pi * n_idx + nrm(s5_lam_shape, 0.01),
        's5_log_dt': jax.random.uniform(next(ks), (DEPTH, 2, S5_GROUPS), f32,
                                        math.log(S5_DT_MIN), math.log(S5_DT_MAX)),
        's5_b_re': nrm((DEPTH, 2, S5_GROUPS, S5_STATE, S5_GROUP_CH), (2 * S5_GROUP_CH) ** -0.5),
        's5_b_im': nrm((DEPTH, 2, S5_GROUPS, S5_STATE, S5_GROUP_CH), (2 * S5_GROUP_CH) ** -0.5),
        's5_c_re': nrm((DEPTH, 2, S5_GROUPS, S5_GROUP_CH, S5_STATE), S5_STATE ** -0.5),
        's5_c_im': nrm((DEPTH, 2, S5_GROUPS, S5_GROUP_CH, S5_STATE), S5_STATE ** -0.5),
        's5_d': nrm((DEPTH, S5_WIDTH), 1.0),
        's5_w_glu': nrm((DEPTH, S5_WIDTH, S5_WIDTH), S5_WIDTH ** -0.5),
        'gla_w1': nrm((DEPTH, 2, D_MODEL, GLA_RANK), D_MODEL ** -0.5),
        'gla_w2': nrm((DEPTH, 2, GLA_RANK, GLA_KW), GLA_RANK ** -0.5),
        'gla_b': nrm((DEPTH, 2, GLA_KW), 0.1),
        'gla_norm_g': 1.0 + nrm((DEPTH, GLA_HEADS, GLA_DV), 0.02),
        'attn_sink': nrm((DEPTH, ATT_HEADS), 0.5),
        'w_branch_s5': nrm((DEPTH, S5_WIDTH, D_MODEL), S5_WIDTH ** -0.5),
        'w_branch_gla': nrm((DEPTH, GLA_VW, D_MODEL), GLA_VW ** -0.5),
        'w_branch_attn': nrm((DEPTH, ATT_QW, D_MODEL), ATT_QW ** -0.5),
        'w_out': nrm((DEPTH, D_MODEL, D_MODEL), D_MODEL ** -0.5),
        'moe_router': nrm((DEPTH, D_MODEL, N_EXPERTS), D_MODEL ** -0.5),
        'moe_w_gate': nrm((DEPTH, N_EXPERTS, D_MODEL, EXPERT_FF), D_MODEL ** -0.5),
        'moe_w_up': nrm((DEPTH, N_EXPERTS, D_MODEL, EXPERT_FF), D_MODEL ** -0.5),
        'moe_w_down': nrm((DEPTH, N_EXPERTS, EXPERT_FF, D_MODEL), EXPERT_FF ** -0.5),
        'final_g': 1.0 + nrm((D_MODEL,), 0.02),
    }


def reference(x, c, ctx, c_ctx, ada_w, ada_b, norm1_g, norm2_g, w_in, s5_lam_re, s5_lam_im, s5_log_dt,
              s5_b_re, s5_b_im, s5_c_re, s5_c_im, s5_d, s5_w_glu, gla_w1, gla_w2, gla_b, gla_norm_g,
              attn_sink, w_branch_s5, w_branch_gla, w_branch_attn, w_out, moe_router, moe_w_gate,
              moe_w_up, moe_w_down, final_g):
    B, L = x.shape[:2]
    n_rows = L // GRID_W
    cos, sin = axial_rope_tables(n_rows, x.dtype)
    xc = ctx
    for l in range(DEPTH):
        last = l == DEPTH - 1
        lp = {
            'w_in': w_in[l], 's5_lam_re': s5_lam_re[l], 's5_lam_im': s5_lam_im[l], 's5_log_dt': s5_log_dt[l],
            's5_b_re': s5_b_re[l], 's5_b_im': s5_b_im[l], 's5_c_re': s5_c_re[l], 's5_c_im': s5_c_im[l],
            's5_d': s5_d[l], 's5_w_glu': s5_w_glu[l], 'gla_w1': gla_w1[l], 'gla_w2': gla_w2[l],
            'gla_b': gla_b[l], 'gla_norm_g': gla_norm_g[l], 'attn_sink': attn_sink[l],
            'w_branch_s5': w_branch_s5[l], 'w_branch_gla': w_branch_gla[l],
            'w_branch_attn': w_branch_attn[l], 'w_out': w_out[l],
        }
        mod = (jax.nn.silu(c) @ ada_w[l] + ada_b[l]).reshape(B, N_MOD, 1, D_MODEL)
        modc = (jax.nn.silu(c_ctx) @ ada_w[l] + ada_b[l]).reshape(N_MOD, D_MODEL)
        h = modulate(rms_norm(x, norm1_g[l]), mod[:, 0], mod[:, 1])
        hc = modulate(rms_norm(xc, norm1_g[l]), modc[0], modc[1])
        y, yc = token_mixer(h, hc, lp, cos, sin, not last)
        x = x + mod[:, 2] * y
        h = modulate(rms_norm(x, norm2_g[l]), mod[:, 3], mod[:, 4])
        x = x + mod[:, 5] * expert_choice_ffn(h, moe_router[l], moe_w_gate[l], moe_w_up[l], moe_w_down[l])
        if not last:
            xc = xc + modc[2] * yc
            hc = modulate(rms_norm(xc, norm2_g[l]), modc[3], modc[4])
            xc = xc + modc[5] * expert_choice_ffn(hc, moe_router[l], moe_w_gate[l], moe_w_up[l], moe_w_down[l])
    return rms_norm(x, final_g)
```

```python
import functools
import math

import jax
import jax.numpy as jnp
from jax import lax
from jax.experimental import pallas as pl
from jax.experimental.pallas import tpu as pltpu

D_MODEL = 2048
SEQ = 8192
DEPTH = 2
GRID_W = 64
CTX_LEN = 256
EPS = 1e-6
N_MOD = 6
S5_WIDTH = 1024
S5_GROUP_CH = 16
S5_GROUPS = S5_WIDTH // S5_GROUP_CH
S5_STATE = 64
GLA_HEADS = 4
GLA_DK = 128
GLA_DV = 256
GLA_KW = GLA_HEADS * GLA_DK
GLA_VW = GLA_HEADS * GLA_DV
GLA_RANK = 16
GLA_TAU = 16.0
GLA_CHUNK = 64
ATT_HEADS = 16
ATT_KV_HEADS = 4
ATT_HEAD_DIM = 64
ATT_QW = ATT_HEADS * ATT_HEAD_DIM
ATT_KVW = ATT_KV_HEADS * ATT_HEAD_DIM
WINDOW = 128
ATT_BLOCK = 128
ROPE_BASE = 10000.0
N_BRANCH = 3
IN_SPLITS = (S5_WIDTH, GLA_KW, GLA_KW, GLA_VW, GLA_VW, ATT_QW, ATT_KVW, ATT_KVW, N_BRANCH * D_MODEL)
IN_WIDTH = sum(IN_SPLITS)
N_EXPERTS = 16
EXPERT_FF = 2048
EC_CAPACITY = 2

F32 = jnp.float32
BF16 = jnp.bfloat16


def _mm_kernel(a_ref, w_ref, o_ref):
    o_ref[...] = jnp.dot(a_ref[...].astype(BF16), w_ref[...].astype(BF16), preferred_element_type=F32)


def pallas_matmul(a, w, tm, tn):
    m, k = a.shape
    _, n = w.shape
    return pl.pallas_call(
        _mm_kernel,
        grid=(m // tm, n // tn),
        in_specs=[pl.BlockSpec((tm, k), lambda i, j: (i, 0)),
                  pl.BlockSpec((k, tn), lambda i, j: (0, j))],
        out_specs=pl.BlockSpec((tm, tn), lambda i, j: (i, j)),
        out_shape=jax.ShapeDtypeStruct((m, n), F32),
        compiler_params=pltpu.CompilerParams(
            dimension_semantics=("arbitrary", "arbitrary"), vmem_limit_bytes=56 * 1024 * 1024),
        name="mm",
    )(a, w)


def rms_norm(x, g):
    x32 = x.astype(jnp.float32)
    r = lax.rsqrt(jnp.mean(x32 * x32, axis=-1, keepdims=True) + EPS)
    return (x32 * r).astype(x.dtype) * g


def modulate(h, shift, scale):
    return h * (1 + scale) + shift


def split_heads(t, n_heads):
    return t.reshape(t.shape[:-1] + (n_heads, t.shape[-1] // n_heads))


def flip_seq(t):
    return jnp.flip(t, axis=1)


def keep_seq(t):
    return t


def split_projection(z):
    offsets = []
    acc = 0
    for w in IN_SPLITS[:-1]:
        acc += w
        offsets.append(acc)
    return jnp.split(z, offsets, axis=-1)


def axial_rope_tables(n_rows, dtype):
    row = jnp.repeat(jnp.arange(n_rows), GRID_W).astype(jnp.float32)
    col = jnp.tile(jnp.arange(GRID_W), n_rows).astype(jnp.float32)
    n_freq = ATT_HEAD_DIM // 4
    inv_freq = ROPE_BASE ** (-jnp.arange(n_freq, dtype=jnp.float32) / n_freq)
    ang = jnp.concatenate([row[:, None] * inv_freq, col[:, None] * inv_freq], axis=-1)
    return jnp.cos(ang).astype(dtype), jnp.sin(ang).astype(dtype)


def apply_rope(t, cos, sin):
    half = t.shape[-1] // 2
    t1, t2 = t[..., :half], t[..., half:]
    cs, sn = cos[None, :, None, :], sin[None, :, None, :]
    return jnp.concatenate([t1 * cs - t2 * sn, t1 * sn + t2 * cs], axis=-1)


def s5_discretize(lam_re, lam_im, log_dt, b_re, b_im):
    dt = jnp.exp(log_dt)[:, None]
    mag = jnp.exp(lam_re * dt)
    ab_re = mag * jnp.cos(lam_im * dt)
    ab_im = mag * jnp.sin(lam_im * dt)
    den = lam_re * lam_re + lam_im * lam_im
    nr = ab_re - 1
    f_re = ((nr * lam_re + ab_im * lam_im) / den)[..., None]
    f_im = ((ab_im * lam_re - nr * lam_im) / den)[..., None]
    bb_re = f_re * b_re - f_im * b_im
    bb_im = f_re * b_im + f_im * b_re
    return ab_re, ab_im, bb_re, bb_im


def complex_affine_combine(e1, e2):
    a1r, a1i, b1r, b1i = e1
    a2r, a2i, b2r, b2i = e2
    return (a2r * a1r - a2i * a1i, a2r * a1i + a2i * a1r,
            a2r * b1r - a2i * b1i + b2r, a2r * b1i + a2i * b1r + b2i)


def s5_states(u, h0_re, h0_im, disc):
    ab_re, ab_im, bb_re, bb_im = disc
    bu_re = jnp.einsum('blgc,gnc->blgn', u, bb_re)
    bu_im = jnp.einsum('blgc,gnc->blgn', u, bb_im)
    bu_re = bu_re.at[:, 0].add(ab_re * h0_re - ab_im * h0_im)
    bu_im = bu_im.at[:, 0].add(ab_re * h0_im + ab_im * h0_re)
    a_re = jnp.broadcast_to(ab_re, bu_re.shape)
    a_im = jnp.broadcast_to(ab_im, bu_im.shape)
    _, _, h_re, h_im = lax.associative_scan(complex_affine_combine, (a_re, a_im, bu_re, bu_im), axis=1)
    return h_re, h_im


def s5_readout(h_re, h_im, c_re, c_im):
    return jnp.einsum('blgn,gcn->blgc', h_re, c_re) - jnp.einsum('blgn,gcn->blgc', h_im, c_im)


def s5_glu(y, w_glu):
    g = jax.nn.gelu(y)
    return g * jax.nn.sigmoid(g @ w_glu)


def s5_branch(u, uc, lp, with_ctx_out):
    B, L, _ = u.shape
    Lc = uc.shape[1]
    ug = u.reshape(B, L, S5_GROUPS, S5_GROUP_CH)
    ucg = uc.reshape(B, Lc, S5_GROUPS, S5_GROUP_CH)
    zero = jnp.zeros((B, S5_GROUPS, S5_STATE), u.dtype)
    y = u * lp['s5_d']
    yc = uc * lp['s5_d'] if with_ctx_out else None
    for d in range(2):
        rev = flip_seq if d == 1 else keep_seq
        disc = s5_discretize(lp['s5_lam_re'][d], lp['s5_lam_im'][d], lp['s5_log_dt'][d],
                             lp['s5_b_re'][d], lp['s5_b_im'][d])
        c_re, c_im = lp['s5_c_re'][d], lp['s5_c_im'][d]
        hc_re, hc_im = s5_states(rev(ucg), zero, zero, disc)
        if with_ctx_out:
            yc = yc + rev(s5_readout(hc_re, hc_im, c_re, c_im)).reshape(B, Lc, S5_WIDTH)
        h_re, h_im = s5_states(rev(ug), hc_re[:, -1], hc_im[:, -1], disc)
        y = y + rev(s5_readout(h_re, h_im, c_re, c_im)).reshape(B, L, S5_WIDTH)
    y = s5_glu(y, lp['s5_w_glu'])
    yc = s5_glu(yc, lp['s5_w_glu']) if with_ctx_out else None
    return y, yc


def gla_log_decay(h, w1, w2, b):
    return jax.nn.log_sigmoid((h @ w1) @ w2 + b) / GLA_TAU


def gla_direction(q, k, v, log_a, s0):
    B, L, H, dk = q.shape
    n = L // GLA_CHUNK

    def chunked(t):
        return jnp.moveaxis(t.reshape(B, n, GLA_CHUNK, H, t.shape[-1]), 1, 0)

    cum = jnp.cumsum(chunked(log_a), axis=2)
    lower = jnp.tril(jnp.ones((GLA_CHUNK, GLA_CHUNK), dtype=bool))[None, :, :, None, None]

    def step(s, inp):
        qc, kc, vc, bc = inp
        o_inter = jnp.einsum('bihd,bhdv->bihv', qc * jnp.exp(bc), s)
        rel = jnp.exp(jnp.where(lower, bc[:, :, None] - bc[:, None, :], -jnp.inf))
        scores = jnp.einsum('bihd,bjhd,bijhd->bhij', qc, kc, rel)
        o_intra = jnp.einsum('bhij,bjhv->bihv', scores, vc)
        b_last = bc[:, -1]
        s_new = (jnp.exp(b_last)[..., None] * s
                 + jnp.einsum('bjhd,bjhv->bhdv', kc * jnp.exp(b_last[:, None] - bc), vc))
        return s_new, o_inter + o_intra

    s_fin, o = lax.scan(step, s0, (chunked(q), chunked(k), chunked(v), cum))
    return jnp.moveaxis(o, 0, 1).reshape(B, L, H, v.shape[-1]), s_fin


def gla_final_state(k, v, log_a):
    cum = jnp.cumsum(log_a, axis=1)
    b_last = cum[:, -1]
    return jnp.einsum('blhd,blhv->bhdv', k * jnp.exp(b_last[:, None] - cum), v)


def gla_branch(zl, zc, h, hc, lp, with_ctx_out):
    q, k, v, r = [split_heads(t, GLA_HEADS) for t in zl]
    qc, kc, vc, rc = [split_heads(t, GLA_HEADS) for t in zc]
    B, L = q.shape[:2]
    Lc = qc.shape[1]
    scale = GLA_DK ** -0.5
    s0 = jnp.zeros((B, GLA_HEADS, GLA_DK, GLA_DV), v.dtype)
    o = jnp.zeros_like(v)
    oc = jnp.zeros_like(vc) if with_ctx_out else None
    for d in range(2):
        rev = flip_seq if d == 1 else keep_seq
        la = split_heads(gla_log_decay(h, lp['gla_w1'][d], lp['gla_w2'][d], lp['gla_b'][d]), GLA_HEADS)
        lac = split_heads(gla_log_decay(hc, lp['gla_w1'][d], lp['gla_w2'][d], lp['gla_b'][d]), GLA_HEADS)
        if with_ctx_out:
            oc_d, sc = gla_direction(rev(qc) * scale, rev(kc), rev(vc), rev(lac), s0)
            oc = oc + rev(oc_d)
        else:
            sc = gla_final_state(rev(kc), rev(vc), rev(lac))
        o_d, _ = gla_direction(rev(q) * scale, rev(k), rev(v), rev(la), sc)
        o = o + rev(o_d)
    y = (rms_norm(o, lp['gla_norm_g']) * jax.nn.silu(r)).reshape(B, L, GLA_VW)
    yc = (rms_norm(oc, lp['gla_norm_g']) * jax.nn.silu(rc)).reshape(B, Lc, GLA_VW) if with_ctx_out else None
    return y, yc


def window_attention(q, k, v, kc, vc, sink):
    B, L = q.shape[:2]
    Lc = kc.shape[1]
    grp = ATT_HEADS // ATT_KV_HEADS
    nb = L // ATT_BLOCK
    nj = 3 * ATT_BLOCK
    qb = q.reshape(B, nb, ATT_BLOCK, ATT_KV_HEADS, grp, ATT_HEAD_DIM) * ATT_HEAD_DIM ** -0.5
    pad = ((0, 0), (ATT_BLOCK, ATT_BLOCK), (0, 0), (0, 0))

    def band(t):
        tb = jnp.pad(t, pad).reshape(B, nb + 2, ATT_BLOCK, ATT_KV_HEADS, ATT_HEAD_DIM)
        return jnp.concatenate([tb[:, :-2], tb[:, 1:-1], tb[:, 2:]], axis=2)

    kb, vb = band(k), band(v)
    qpos = jnp.arange(L).reshape(nb, ATT_BLOCK, 1)
    kpos = ((jnp.arange(nb) - 1) * ATT_BLOCK)[:, None, None] + jnp.arange(nj)[None, None, :]
    valid = (jnp.abs(kpos - qpos) <= WINDOW) & (kpos >= 0) & (kpos < L)
    s_lat = jnp.einsum('bnqkgd,bnjkd->bnkgqj', qb, kb).astype(jnp.float32)
    s_lat = jnp.where(valid[None, :, None, None], s_lat, -jnp.inf)
    s_ctx = jnp.einsum('bnqkgd,bckd->bnkgqc', qb, kc).astype(jnp.float32)
    s_sink = jnp.broadcast_to(sink.astype(jnp.float32).reshape(1, 1, ATT_KV_HEADS, grp, 1, 1),
                              s_lat.shape[:-1] + (1,))
    p = jax.nn.softmax(jnp.concatenate([s_lat, s_ctx, s_sink], axis=-1), axis=-1).astype(v.dtype)
    o = (jnp.einsum('bnkgqj,bnjkd->bnqkgd', p[..., :nj], vb)
         + jnp.einsum('bnkgqc,bckd->bnqkgd', p[..., nj:nj + Lc], vc))
    return o.reshape(B, L, ATT_QW)


def context_attention(qc, kc, vc, sink):
    B, Lc = qc.shape[:2]
    grp = ATT_HEADS // ATT_KV_HEADS
    qg = qc.reshape(B, Lc, ATT_KV_HEADS, grp, ATT_HEAD_DIM) * ATT_HEAD_DIM ** -0.5
    s = jnp.einsum('bqkgd,bckd->bkgqc', qg, kc).astype(jnp.float32)
    s_sink = jnp.broadcast_to(sink.astype(jnp.float32).reshape(1, ATT_KV_HEADS, grp, 1, 1), s.shape[:-1] + (1,))
    p = jax.nn.softmax(jnp.concatenate([s, s_sink], axis=-1), axis=-1).astype(vc.dtype)
    o = jnp.einsum('bkgqc,bckd->bqkgd', p[..., :Lc], vc)
    return o.reshape(B, Lc, ATT_QW)


def merge_branches(gates, y_s5, y_gla, y_att, lp):
    g = jax.nn.sigmoid(split_heads(gates, N_BRANCH))
    m = (g[..., 0, :] * (y_s5 @ lp['w_branch_s5'])
         + g[..., 1, :] * (y_gla @ lp['w_branch_gla'])
         + g[..., 2, :] * (y_att @ lp['w_branch_attn']))
    return m @ lp['w_out']


def token_mixer(h, hc, lp, cos, sin, with_ctx_out):
    hh = jnp.concatenate([hc[None], h], axis=1)[0]
    z = pallas_matmul(hh, lp['w_in'], 1056, 512)
    zc, zl = z[None, :CTX_LEN], z[None, CTX_LEN:]
    u, qg, kg, vg, rg, qa, ka, va, gates = split_projection(zl)
    uc, qgc, kgc, vgc, rgc, qac, kac, vac, gatesc = split_projection(zc)
    y_s5, yc_s5 = s5_branch(u, uc, lp, with_ctx_out)
    y_gla, yc_gla = gla_branch((qg, kg, vg, rg), (qgc, kgc, vgc, rgc), h, hc[None], lp, with_ctx_out)
    kc_h = split_heads(kac, ATT_KV_HEADS)
    vc_h = split_heads(vac, ATT_KV_HEADS)
    q = apply_rope(split_heads(qa, ATT_HEADS), cos, sin)
    k = apply_rope(split_heads(ka, ATT_KV_HEADS), cos, sin)
    y_att = window_attention(q, k, split_heads(va, ATT_KV_HEADS), kc_h, vc_h, lp['attn_sink'])
    y = merge_branches(gates, y_s5, y_gla, y_att, lp)
    if not with_ctx_out:
        return y, None
    yc_att = context_attention(split_heads(qac, ATT_HEADS), kc_h, vc_h, lp['attn_sink'])
    return y, merge_branches(gatesc, yc_s5, yc_gla, yc_att, lp)


def expert_choice_ffn(h, router, w_gate, w_up, w_down):
    B, N, _ = h.shape
    cap = EC_CAPACITY * N // N_EXPERTS
    aff = jax.nn.softmax((h @ router).astype(jnp.float32), axis=-1)
    gate, idx = lax.top_k(jnp.swapaxes(aff, 1, 2), cap)
    bidx = jnp.arange(B)[:, None, None]
    xs = h[bidx, idx]
    a = jnp.einsum('becd,edf->becf', xs, w_gate)
    up = jnp.einsum('becd,edf->becf', xs, w_up)
    ye = jnp.einsum('becf,efd->becd', jax.nn.silu(a) * up, w_down) * gate[..., None].astype(h.dtype)
    return jnp.zeros_like(h).at[bidx, idx].add(ye)


def kernel(x, c, ctx, c_ctx, ada_w, ada_b, norm1_g, norm2_g, w_in, s5_lam_re, s5_lam_im, s5_log_dt,
           s5_b_re, s5_b_im, s5_c_re, s5_c_im, s5_d, s5_w_glu, gla_w1, gla_w2, gla_b, gla_norm_g,
           attn_sink, w_branch_s5, w_branch_gla, w_branch_attn, w_out, moe_router, moe_w_gate,
           moe_w_up, moe_w_down, final_g):
    B, L = x.shape[:2]
    n_rows = L // GRID_W
    cos, sin = axial_rope_tables(n_rows, x.dtype)
    xc = ctx
    for l in range(DEPTH):
        last = l == DEPTH - 1
        lp = {
            'w_in': w_in[l], 's5_lam_re': s5_lam_re[l], 's5_lam_im': s5_lam_im[l], 's5_log_dt': s5_log_dt[l],
            's5_b_re': s5_b_re[l], 's5_b_im': s5_b_im[l], 's5_c_re': s5_c_re[l], 's5_c_im': s5_c_im[l],
            's5_d': s5_d[l], 's5_w_glu': s5_w_glu[l], 'gla_w1': gla_w1[l], 'gla_w2': gla_w2[l],
            'gla_b': gla_b[l], 'gla_norm_g': gla_norm_g[l], 'attn_sink': attn_sink[l],
            'w_branch_s5': w_branch_s5[l], 'w_branch_gla': w_branch_gla[l],
            'w_branch_attn': w_branch_attn[l], 'w_out': w_out[l],
        }
        mod = (jax.nn.silu(c) @ ada_w[l] + ada_b[l]).reshape(B, N_MOD, 1, D_MODEL)
        modc = (jax.nn.silu(c_ctx) @ ada_w[l] + ada_b[l]).reshape(N_MOD, D_MODEL)
        h = modulate(rms_norm(x, norm1_g[l]), mod[:, 0], mod[:, 1])
        hc = modulate(rms_norm(xc, norm1_g[l]), modc[0], modc[1])
        y, yc = token_mixer(h, hc[0], lp, cos, sin, not last)
        x = x + mod[:, 2] * y
        h = modulate(rms_norm(x, norm2_g[l]), mod[:, 3], mod[:, 4])
        x = x + mod[:, 5] * expert_choice_ffn(h, moe_router[l], moe_w_gate[l], moe_w_up[l], moe_w_down[l])
        if not last:
            xc = xc + modc[2] * yc
            hc = modulate(rms_norm(xc, norm2_g[l]), modc[3], modc[4])
            xc = xc + modc[5] * expert_choice_ffn(hc, moe_router[l], moe_w_gate[l], moe_w_up[l], moe_w_down[l])
    return rms_norm(x, final_g)
```

```python
import functools
import math

import jax
import jax.numpy as jnp
from jax import lax
from jax.experimental import pallas as pl
from jax.experimental.pallas import tpu as pltpu

D_MODEL = 2048
DEPTH = 2
GRID_W = 64
EPS = 1e-6
N_MOD = 6
S5_WIDTH = 1024
S5_GROUP_CH = 16
S5_GROUPS = S5_WIDTH // S5_GROUP_CH
S5_STATE = 64
GLA_HEADS = 4
GLA_DK = 128
GLA_DV = 256
GLA_KW = GLA_HEADS * GLA_DK
GLA_VW = GLA_HEADS * GLA_DV
GLA_RANK = 16
GLA_TAU = 16.0
GLA_CHUNK = 64
ATT_HEADS = 16
ATT_KV_HEADS = 4
ATT_HEAD_DIM = 64
ATT_QW = ATT_HEADS * ATT_HEAD_DIM
ATT_KVW = ATT_KV_HEADS * ATT_HEAD_DIM
WINDOW = 128
ATT_BLOCK = 128
ROPE_BASE = 10000.0
N_BRANCH = 3
IN_SPLITS = (S5_WIDTH, GLA_KW, GLA_KW, GLA_VW, GLA_VW, ATT_QW, ATT_KVW, ATT_KVW, N_BRANCH * D_MODEL)
IN_WIDTH = sum(IN_SPLITS)
N_EXPERTS = 16
EXPERT_FF = 2048
EC_CAPACITY = 2

OFF_U = 0
OFF_GQ = OFF_U + S5_WIDTH
OFF_GK = OFF_GQ + GLA_KW
OFF_GV = OFF_GK + GLA_KW
OFF_GR = OFF_GV + GLA_VW
OFF_AQ = OFF_GR + GLA_VW
OFF_AK = OFF_AQ + ATT_QW
OFF_AV = OFF_AK + ATT_KVW
OFF_GATE = OFF_AV + ATT_KVW

LANES = 128
S5_T = 16
S5_PAIRS = S5_GROUPS // 2
S5_PAIR_W = 2 * S5_T * S5_GROUP_CH
S5_SCAN_PAIRS = 8
TOK_TILE = 256
MOE_WIN = 64
FF_TILE = 256
VMEM_MB = 56

F32 = jnp.float32
BF16 = jnp.bfloat16
HI = lax.Precision.HIGHEST


def _cp(sem, mb=VMEM_MB):
    return pltpu.CompilerParams(dimension_semantics=sem, vmem_limit_bytes=mb << 20)


def _dot(a, b):
    return jnp.dot(a, b, preferred_element_type=F32)


def _dot_nt(a, b):
    return lax.dot_general(a, b, (((1,), (1,)), ((), ())), preferred_element_type=F32)


def _dot_tn(a, b):
    return lax.dot_general(a, b, (((0,), (0,)), ((), ())), preferred_element_type=F32)


def _pick(n, cands):
    for c in cands:
        if n % c == 0:
            return c
    raise ValueError(f"no tile for {n} in {cands}")


def _sigmoid(x):
    return 1.0 / (1.0 + jnp.exp(-x))


def _row_mods(mod_ref, k, is_ctx):
    lo, hi = k * D_MODEL, (k + 1) * D_MODEL
    return jnp.where(is_ctx, mod_ref[1:2, lo:hi], mod_ref[0:1, lo:hi])


def _ada_kernel(c_ref, w_ref, b_ref, o_ref):
    cc = c_ref[...]
    o_ref[0] = jnp.dot(cc * _sigmoid(cc), w_ref[0], preferred_element_type=F32, precision=HI) + b_ref[0]


def ada_mods(c8, ada_w, ada_b):
    tn = 1024
    width = N_MOD * D_MODEL
    return pl.pallas_call(
        _ada_kernel,
        grid=(DEPTH, width // tn),
        in_specs=[pl.BlockSpec((8, D_MODEL), lambda l, j: (0, 0)),
                  pl.BlockSpec((1, D_MODEL, tn), lambda l, j: (l, 0, j)),
                  pl.BlockSpec((1, 1, tn), lambda l, j: (l, 0, j))],
        out_specs=pl.BlockSpec((1, 8, tn), lambda l, j: (l, 0, j)),
        out_shape=jax.ShapeDtypeStruct((DEPTH, 8, width), F32),
        compiler_params=_cp(("arbitrary", "arbitrary")),
        name="ada_mods",
    )(c8, ada_w, ada_b.reshape(DEPTH, 1, width))


def _in_kernel(x_ref, g_ref, mod_ref, w_ref, w1_ref, z_ref, lr_ref, h_scr, *, ctx_len, tm):
    i = pl.program_id(0)
    j = pl.program_id(1)

    @pl.when(j == 0)
    def _():
        x = x_ref[...]
        r = lax.rsqrt(jnp.mean(x * x, axis=-1, keepdims=True) + EPS)
        hn = (x * r) * g_ref[0]
        is_ctx = (i * tm + lax.broadcasted_iota(jnp.int32, (tm, 1), 0)) < ctx_len
        hb = (hn * (1.0 + _row_mods(mod_ref, 1, is_ctx)) + _row_mods(mod_ref, 0, is_ctx)).astype(BF16)
        h_scr[...] = hb
        lr_ref[...] = _dot(hb, w1_ref[...].astype(BF16))

    z_ref[...] = _dot(h_scr[...], w_ref[...].astype(BF16))


def in_projection(xs, norm_g, mods, w_in, w1cat, l, ctx_len):
    n = xs.shape[0]
    tm = _pick(n, (1056, 640, 256))
    tn = 512
    return pl.pallas_call(
        functools.partial(_in_kernel, ctx_len=ctx_len, tm=tm),
        grid=(n // tm, IN_WIDTH // tn),
        in_specs=[pl.BlockSpec((tm, D_MODEL), lambda i, j: (i, 0)),
                  pl.BlockSpec((1, 1, D_MODEL), lambda i, j: (l, 0, 0)),
                  pl.BlockSpec((None, 8, N_MOD * D_MODEL), lambda i, j: (l, 0, 0)),
                  pl.BlockSpec((None, D_MODEL, tn), lambda i, j: (l, 0, j)),
                  pl.BlockSpec((D_MODEL, LANES), lambda i, j: (0, 0))],
        out_specs=[pl.BlockSpec((tm, tn), lambda i, j: (i, j)),
                   pl.BlockSpec((tm, LANES), lambda i, j: (i, 0))],
        out_shape=[jax.ShapeDtypeStruct((n, IN_WIDTH), F32),
                   jax.ShapeDtypeStruct((n, LANES), F32)],
        scratch_shapes=[pltpu.VMEM((tm, D_MODEL), BF16)],
        compiler_params=_cp(("arbitrary", "arbitrary")),
        name="in_projection",
    )(xs, norm_g.reshape(DEPTH, 1, D_MODEL), mods, w_in, w1cat)


def _s5_direction_weights(lam_re, lam_im, log_dt, b_re, b_im, c_re, c_im, rev):
    t_len, g_n, n_st, ch = S5_T, S5_GROUPS, S5_STATE, S5_GROUP_CH
    dt = jnp.exp(log_dt)[:, None]
    mag = jnp.exp(lam_re * dt)
    ab_re = mag * jnp.cos(lam_im * dt)
    ab_im = mag * jnp.sin(lam_im * dt)
    den = lam_re * lam_re + lam_im * lam_im
    nr = ab_re - 1
    f_re = ((nr * lam_re + ab_im * lam_im) / den)[..., None]
    f_im = ((ab_im * lam_re - nr * lam_im) / den)[..., None]
    bb_re = f_re * b_re - f_im * b_im
    bb_im = f_re * b_im + f_im * b_re
    tau = jnp.arange(t_len + 1, dtype=F32)[:, None, None]
    pmag = jnp.exp(lam_re * dt * tau)
    ang = lam_im * dt * tau
    p_re, p_im = pmag * jnp.cos(ang), pmag * jnp.sin(ang)

    def c_times(pr, pi):
        return (c_re[None] * pr[:, :, None, :] - c_im[None] * pi[:, :, None, :],
                c_re[None] * pi[:, :, None, :] + c_im[None] * pr[:, :, None, :])

    cp_re, cp_im = c_times(p_re[:t_len], p_im[:t_len])
    kern = (jnp.einsum('tgon,gni->tgoi', cp_re, bb_re, precision=HI)
            - jnp.einsum('tgon,gni->tgoi', cp_im, bb_im, precision=HI))
    s_idx = jnp.arange(t_len)[:, None]
    t_idx = jnp.arange(t_len)[None, :]
    lag = (s_idx - t_idx) if rev else (t_idx - s_idx)
    kg = jnp.where((lag >= 0)[:, :, None, None, None], kern[jnp.clip(lag, 0, t_len - 1)], 0.0)
    w_toep = kg.transpose(2, 0, 4, 1, 3).reshape(g_n, t_len * ch, t_len * ch)

    e_idx = jnp.arange(t_len) if rev else t_len - 1 - jnp.arange(t_len)
    pe_re, pe_im = p_re[e_idx][..., None], p_im[e_idx][..., None]
    wb_re = (pe_re * bb_re[None] - pe_im * bb_im[None]).transpose(1, 0, 3, 2).reshape(g_n, t_len * ch, n_st)
    wb_im = (pe_re * bb_im[None] + pe_im * bb_re[None]).transpose(1, 0, 3, 2).reshape(g_n, t_len * ch, n_st)

    f_idx = (t_len - jnp.arange(t_len)) if rev else (jnp.arange(t_len) + 1)
    e_re, e_im = c_times(p_re[f_idx], p_im[f_idx])
    wc_re = e_re.transpose(1, 3, 0, 2).reshape(g_n, n_st, t_len * ch)
    wc_im = (-e_im).transpose(1, 3, 0, 2).reshape(g_n, n_st, t_len * ch)

    eye2 = jnp.eye(2, dtype=F32)

    def pair_rows(w):
        r, c = w.shape[1:]
        return jnp.einsum('pgrn,gh->pgrhn', w.reshape(S5_PAIRS, 2, r, c), eye2).reshape(S5_PAIRS, 2 * r, 2 * c)

    wb_pair = jnp.concatenate([pair_rows(wb_re), pair_rows(wb_im)], axis=-1)
    wc_pair = jnp.concatenate([pair_rows(wc_re), pair_rows(wc_im)], axis=1)
    a_pair = jnp.concatenate([p_re[t_len].reshape(S5_PAIRS, 2 * n_st),
                              p_im[t_len].reshape(S5_PAIRS, 2 * n_st)], axis=-1)
    return w_toep.astype(BF16), wb_pair.astype(BF16), wc_pair.astype(BF16), a_pair


def s5_weights(lam_re, lam_im, log_dt, b_re, b_im, c_re, c_im):
    per_dir = [_s5_direction_weights(lam_re[d], lam_im[d], log_dt[d], b_re[d], b_im[d], c_re[d], c_im[d], d == 1)
               for d in range(2)]
    return tuple(jnp.stack(t) for t in zip(*per_dir))


def _s5_state_in_kernel(up_ref, wb_ref, v_ref):
    v_ref[0] = _dot(up_ref[0].astype(BF16), wb_ref[0, 0])


def _s5_scan_kernel(v_ref, a_ref, sp_ref, *, nch, ncc):
    rev = pl.program_id(0) == 1
    ar = a_ref[0, :, 0:LANES]
    ai = a_ref[0, :, LANES:2 * LANES]

    def body(i, carry):
        sr, si = carry
        k = jnp.where(rev, jnp.where(i < ncc, ncc - 1 - i, nch + ncc - 1 - i), i)
        v = v_ref[0, k]
        sp_ref[0, k] = jnp.concatenate([sr, si], axis=-1)
        return ar * sr - ai * si + v[:, 0:LANES], ar * si + ai * sr + v[:, LANES:2 * LANES]

    zero = jnp.zeros((S5_SCAN_PAIRS, LANES), F32)
    lax.fori_loop(0, nch, body, (zero, zero))


def _s5_out_kernel(up_ref, sp_ref, wt_ref, wc_ref, dp_ref, y_ref):
    u = up_ref[0]
    ub = u.astype(BF16)
    half = S5_PAIR_W // 2
    y = u * dp_ref[0]
    for d in range(2):
        y = y + _dot(sp_ref[d].astype(BF16), wc_ref[d, 0])
        y = y + jnp.concatenate([_dot(ub[:, :half], wt_ref[d, 0]), _dot(ub[:, half:], wt_ref[d, 1])], axis=-1)
    y_ref[0] = y


def s5_mix(u, weights, d_skip, ctx_len):
    n = u.shape[0]
    nch, ncc = n // S5_T, ctx_len // S5_T
    w_toep, wb_pair, wc_pair, a_pair = weights
    up = u.reshape(nch, S5_T, S5_PAIRS, 2, S5_GROUP_CH).transpose(2, 0, 3, 1, 4).reshape(S5_PAIRS, nch, S5_PAIR_W)
    sw = 2 * LANES
    v = pl.pallas_call(
        _s5_state_in_kernel,
        grid=(S5_PAIRS, 2),
        in_specs=[pl.BlockSpec((1, nch, S5_PAIR_W), lambda p, d: (p, 0, 0)),
                  pl.BlockSpec((1, 1, S5_PAIR_W, sw), lambda p, d: (d, p, 0, 0))],
        out_specs=pl.BlockSpec((1, nch, sw), lambda p, d: (d, 0, p)),
        out_shape=jax.ShapeDtypeStruct((2, nch, S5_PAIRS * sw), F32),
        compiler_params=_cp(("arbitrary", "arbitrary")),
        name="s5_state_in",
    )(up, wb_pair)
    sp = pl.pallas_call(
        functools.partial(_s5_scan_kernel, nch=nch, ncc=ncc),
        grid=(2, S5_PAIRS // S5_SCAN_PAIRS),
        in_specs=[pl.BlockSpec((1, nch, S5_SCAN_PAIRS, sw), lambda d, b: (d, 0, b, 0)),
                  pl.BlockSpec((1, S5_SCAN_PAIRS, sw), lambda d, b: (d, b, 0))],
        out_specs=pl.BlockSpec((1, nch, S5_SCAN_PAIRS, sw), lambda d, b: (d, 0, b, 0)),
        out_shape=jax.ShapeDtypeStruct((2, nch, S5_PAIRS, sw), F32),
        compiler_params=_cp(("arbitrary", "arbitrary")),
        name="s5_scan",
    )(v.reshape(2, nch, S5_PAIRS, sw), a_pair)
    dp = jnp.tile(d_skip.reshape(S5_PAIRS, 2, 1, S5_GROUP_CH), (1, 1, S5_T, 1)).reshape(S5_PAIRS, 1, S5_PAIR_W)
    yp = pl.pallas_call(
        _s5_out_kernel,
        grid=(S5_PAIRS,),
        in_specs=[pl.BlockSpec((1, nch, S5_PAIR_W), lambda p: (p, 0, 0)),
                  pl.BlockSpec((2, nch, sw), lambda p: (0, 0, p)),
                  pl.BlockSpec((2, 2, S5_PAIR_W // 2, S5_PAIR_W // 2), lambda p: (0, p, 0, 0)),
                  pl.BlockSpec((2, 1, sw, S5_PAIR_W), lambda p: (0, p, 0, 0)),
                  pl.BlockSpec((1, 1, S5_PAIR_W), lambda p: (p, 0, 0))],
        out_specs=pl.BlockSpec((1, nch, S5_PAIR_W), lambda p: (p, 0, 0)),
        out_shape=jax.ShapeDtypeStruct((S5_PAIRS, nch, S5_PAIR_W), F32),
        compiler_params=_cp(("arbitrary",)),
        name="s5_out",
    )(up, sp.reshape(2, nch, S5_PAIRS * sw), w_toep, wc_pair, dp)
    return yp.reshape(S5_PAIRS, nch, 2, S5_T, S5_GROUP_CH).transpose(1, 3, 0, 2, 4).reshape(n, S5_WIDTH)


def _s5_glu_kernel(y_ref, w_ref, o_ref):
    y = y_ref[...]
    g = 0.5 * y * (1.0 + jnp.tanh(math.sqrt(2.0 / math.pi) * (y + 0.044715 * (y * y * y))))
    o_ref[...] = (g * _sigmoid(_dot(g.astype(BF16), w_ref[...].astype(BF16)))).astype(BF16)


def s5_glu(y, w_glu, l):
    n = y.shape[0]
    tm = _pick(n, (1056, 640, 256))
    return pl.pallas_call(
        _s5_glu_kernel,
        grid=(n // tm,),
        in_specs=[pl.BlockSpec((tm, S5_WIDTH), lambda i: (i, 0)),
                  pl.BlockSpec((None, S5_WIDTH, S5_WIDTH), lambda i: (l, 0, 0))],
        out_specs=pl.BlockSpec((tm, S5_WIDTH), lambda i: (i, 0)),
        out_shape=jax.ShapeDtypeStruct((n, S5_WIDTH), BF16),
        compiler_params=_cp(("arbitrary",)),
        name="s5_glu",
    )(y, w_glu)


def _gla_kernel(q_ref, k_ref, v_ref, lr_ref, w2_ref, b_ref, o_ref, st_scr, *, rev):
    c = GLA_CHUNK

    @pl.when(pl.program_id(0) == 0)
    def _():
        st_scr[...] = jnp.zeros_like(st_scr)

    x = _dot(lr_ref[...].astype(BF16), w2_ref[0].astype(BF16)) + b_ref[0]
    la = (jnp.minimum(x, 0.0) - jnp.log1p(jnp.exp(-jnp.abs(x)))) * (1.0 / GLA_TAU)
    ri = lax.broadcasted_iota(jnp.int32, (c, c), 0)
    ci = lax.broadcasted_iota(jnp.int32, (c, c), 1)
    keep = (ci >= ri) if rev else (ci <= ri)
    tri = jnp.where(keep, 1.0, 0.0).astype(BF16)
    hi = la.astype(BF16)
    r1 = la - hi.astype(F32)
    mid = r1.astype(BF16)
    lo = (r1 - mid.astype(F32)).astype(BF16)
    bc = _dot(tri, hi) + _dot(tri, mid) + _dot(tri, lo)
    last, ref_row = (0, c // 2) if rev else (c - 1, c // 2 - 1)
    btot = bc[last:last + 1]
    bmid = bc[ref_row:ref_row + 1]
    qs = q_ref[...] * (GLA_DK ** -0.5)
    kk = k_ref[...]
    q_mid = (qs * jnp.exp(bc - bmid)).astype(BF16)
    k_mid = (kk * jnp.exp(bmid - bc)).astype(BF16)
    q_in = (qs * jnp.exp(bc)).astype(BF16)
    k_out = (kk * jnp.exp(btot - bc)).astype(BF16)
    decay = jnp.exp(btot)
    for h in range(GLA_HEADS):
        ks = slice(h * GLA_DK, (h + 1) * GLA_DK)
        vs = slice(h * GLA_DV, (h + 1) * GLA_DV)
        vb = v_ref[:, vs].astype(BF16)
        sc = jnp.where(keep, _dot_nt(q_mid[:, ks], k_mid[:, ks]), 0.0).astype(BF16)
        st = st_scr[h]
        o_ref[:, vs] = _dot(sc, vb) + _dot_nt(q_in[:, ks], st.astype(BF16))
        st_scr[h] = decay[:, ks] * st + _dot_tn(vb, k_out[:, ks])


def gla_direction(z, lr, w2pad, bias, d, ctx_len):
    n = z.shape[0]
    c = GLA_CHUNK
    nch, ncc = n // c, ctx_len // c
    if d == 0:
        row = lambda i: i
    else:
        row = lambda i: jnp.where(i < ncc, ncc - 1 - i, nch + ncc - 1 - i)
    return pl.pallas_call(
        functools.partial(_gla_kernel, rev=d == 1),
        grid=(nch,),
        in_specs=[pl.BlockSpec((c, GLA_KW), lambda i: (row(i), OFF_GQ // GLA_KW)),
                  pl.BlockSpec((c, GLA_KW), lambda i: (row(i), OFF_GK // GLA_KW)),
                  pl.BlockSpec((c, GLA_VW), lambda i: (row(i), OFF_GV // GLA_VW)),
                  pl.BlockSpec((c, LANES), lambda i: (row(i), 0)),
                  pl.BlockSpec((1, LANES, GLA_KW), lambda i: (d, 0, 0)),
                  pl.BlockSpec((1, 1, GLA_KW), lambda i: (d, 0, 0))],
        out_specs=pl.BlockSpec((c, GLA_VW), lambda i: (row(i), 0)),
        out_shape=jax.ShapeDtypeStruct((n, GLA_VW), F32),
        scratch_shapes=[pltpu.VMEM((GLA_HEADS, GLA_DV, GLA_DK), F32)],
        compiler_params=_cp(("arbitrary",)),
        name=f"gla_dir{d}",
    )(z, z, z, lr, w2pad, bias)


def _gla_post_kernel(of_ref, ob_ref, r_ref, g_ref, y_ref):
    for h in range(GLA_HEADS):
        vs = slice(h * GLA_DV, (h + 1) * GLA_DV)
        o = of_ref[:, vs] + ob_ref[:, vs]
        rn = lax.rsqrt(jnp.mean(o * o, axis=-1, keepdims=True) + EPS)
        r = r_ref[:, vs]
        y_ref[:, vs] = ((o * rn) * g_ref[0, :, vs] * (r * _sigmoid(r))).astype(BF16)


def gla_post(o_f, o_b, z, norm_g, l):
    n = z.shape[0]
    tm = _pick(n, (1056, 640, 256))
    return pl.pallas_call(
        _gla_post_kernel,
        grid=(n // tm,),
        in_specs=[pl.BlockSpec((tm, GLA_VW), lambda i: (i, 0)),
                  pl.BlockSpec((tm, GLA_VW), lambda i: (i, 0)),
                  pl.BlockSpec((tm, GLA_VW), lambda i: (i, OFF_GR // GLA_VW)),
                  pl.BlockSpec((1, 1, GLA_VW), lambda i: (l, 0, 0))],
        out_specs=pl.BlockSpec((tm, GLA_VW), lambda i: (i, 0)),
        out_shape=jax.ShapeDtypeStruct((n, GLA_VW), BF16),
        compiler_params=_cp(("arbitrary",)),
        name="gla_post",
    )(o_f, o_b, z, norm_g.reshape(DEPTH, 1, GLA_VW))


def _att_prep_kernel(q_ref, k_ref, v_ref, cos_ref, sin_ref, qo_ref, ko_ref, vo_ref):
    cos = cos_ref[...]
    sin = sin_ref[...]
    lane = lax.broadcasted_iota(jnp.int32, (1, LANES), 1)
    first_half = (lane % ATT_HEAD_DIM) < (ATT_HEAD_DIM // 2)
    low_head = lane < ATT_HEAD_DIM

    def rope(xc):
        rot = jnp.where(first_half, pltpu.roll(xc, LANES - ATT_HEAD_DIM // 2, 1), pltpu.roll(xc, ATT_HEAD_DIM // 2, 1))
        return xc * cos + rot * sin

    def dup(xc, parity):
        sw = pltpu.roll(xc, ATT_HEAD_DIM, 1)
        return jnp.where(low_head, xc, sw) if parity == 0 else jnp.where(low_head, sw, xc)

    for cidx in range(ATT_QW // LANES):
        cs = slice(cidx * LANES, (cidx + 1) * LANES)
        qo_ref[:, cs] = (rope(q_ref[:, cs]) * (ATT_HEAD_DIM ** -0.5)).astype(BF16)
    for kh in range(ATT_KV_HEADS):
        ps = slice((kh // 2) * LANES, (kh // 2 + 1) * LANES)
        os_ = slice(kh * LANES, (kh + 1) * LANES)
        ko_ref[:, os_] = dup(rope(k_ref[:, ps]), kh % 2).astype(BF16)
        vo_ref[:, os_] = dup(v_ref[:, ps], kh % 2).astype(BF16)


def att_prep(z, cos_t, sin_t):
    n = z.shape[0]
    tm = 256
    kw = ATT_KV_HEADS * LANES
    return pl.pallas_call(
        _att_prep_kernel,
        grid=(n // tm,),
        in_specs=[pl.BlockSpec((tm, ATT_QW), lambda i: (i, OFF_AQ // ATT_QW)),
                  pl.BlockSpec((tm, ATT_KVW), lambda i: (i, OFF_AK // ATT_KVW)),
                  pl.BlockSpec((tm, ATT_KVW), lambda i: (i, OFF_AV // ATT_KVW)),
                  pl.BlockSpec((tm, LANES), lambda i: (i, 0)),
                  pl.BlockSpec((tm, LANES), lambda i: (i, 0))],
        out_specs=[pl.BlockSpec((tm, ATT_QW), lambda i: (i, 0)),
                   pl.BlockSpec((tm, kw), lambda i: (i, 0)),
                   pl.BlockSpec((tm, kw), lambda i: (i, 0))],
        out_shape=[jax.ShapeDtypeStruct((n, ATT_QW), BF16),
                   jax.ShapeDtypeStruct((n, kw), BF16),
                   jax.ShapeDtypeStruct((n, kw), BF16)],
        compiler_params=_cp(("arbitrary",)),
        name="att_prep",
    )(z, z, z, cos_t, sin_t)


def _att_heads(q_ref, k_all, v_all, bias, sink_ref, o_ref):
    lane = lax.broadcasted_iota(jnp.int32, (1, LANES), 1)
    low_head = lane < ATT_HEAD_DIM
    grp = ATT_HEADS // ATT_KV_HEADS
    for kh in range(ATT_KV_HEADS):
        k2 = k_all[:, kh * LANES:(kh + 1) * LANES]
        v2 = v_all[:, kh * LANES:(kh + 1) * LANES]
        for pr in range(grp // 2):
            pair = kh * (grp // 2) + pr
            qp = q_ref[:, pair * LANES:(pair + 1) * LANES]
            outs = []
            for par in range(2):
                h = 2 * pair + par
                qm = jnp.where(low_head if par == 0 else jnp.logical_not(low_head), qp, jnp.zeros_like(qp))
                s = _dot_nt(qm, k2)
                if bias is not None:
                    s = s + bias
                sk = sink_ref[h]
                m = jnp.maximum(jnp.max(s, axis=-1, keepdims=True), sk)
                p = jnp.exp(s - m)
                den = jnp.sum(p, axis=-1, keepdims=True) + jnp.exp(sk - m)
                outs.append(_dot((p / den).astype(BF16), v2))
            o_ref[:, pair * LANES:(pair + 1) * LANES] = jnp.where(low_head, outs[0], outs[1]).astype(BF16)


def _att_kernel(sink_ref, q_ref, kp_ref, kc_ref, kn_ref, kx_ref, vp_ref, vc_ref, vn_ref, vx_ref, o_ref, *, seq):
    i = pl.program_id(0)
    blk = ATT_BLOCK
    k_all = jnp.concatenate([kp_ref[...], kc_ref[...], kn_ref[...], kx_ref[...]], axis=0)
    v_all = jnp.concatenate([vp_ref[...], vc_ref[...], vn_ref[...], vx_ref[...]], axis=0)
    nctx = kx_ref.shape[0]
    qpos = i * blk + lax.broadcasted_iota(jnp.int32, (blk, 1), 0)
    kpos = (i - 1) * blk + lax.broadcasted_iota(jnp.int32, (1, 3 * blk), 1)
    valid = (jnp.abs(kpos - qpos) <= WINDOW) & (kpos >= 0) & (kpos < seq)
    bias = jnp.concatenate([jnp.where(valid, 0.0, -jnp.inf), jnp.zeros((blk, nctx), F32)], axis=1)
    _att_heads(q_ref, k_all, v_all, bias, sink_ref, o_ref)


def _att_ctx_kernel(sink_ref, q_ref, kx_ref, vx_ref, o_ref):
    _att_heads(q_ref, kx_ref[...], vx_ref[...], None, sink_ref, o_ref)


def attention(qr, k2, v2, sink, ctx_len):
    n = qr.shape[0]
    blk = ATT_BLOCK
    seq = n - ctx_len
    nb, cb = seq // blk, ctx_len // blk
    kw = ATT_KV_HEADS * LANES
    prev = lambda i, s: (jnp.maximum(i - 1, 0) + cb, 0)
    cur = lambda i, s: (i + cb, 0)
    nxt = lambda i, s: (jnp.minimum(i + 1, nb - 1) + cb, 0)
    cx = lambda i, s: (0, 0)
    band = [pl.BlockSpec((blk, kw), prev), pl.BlockSpec((blk, kw), cur), pl.BlockSpec((blk, kw), nxt),
            pl.BlockSpec((ctx_len, kw), cx)]
    y_lat = pl.pallas_call(
        functools.partial(_att_kernel, seq=seq),
        grid_spec=pltpu.PrefetchScalarGridSpec(
            num_scalar_prefetch=1, grid=(nb,),
            in_specs=[pl.BlockSpec((blk, ATT_QW), cur)] + band + band,
            out_specs=pl.BlockSpec((blk, ATT_QW), lambda i, s: (i, 0))),
        out_shape=jax.ShapeDtypeStruct((seq, ATT_QW), BF16),
        compiler_params=_cp(("arbitrary",)),
        name="att_window",
    )(sink, qr, k2, k2, k2, k2, v2, v2, v2, v2)
    y_ctx = pl.pallas_call(
        _att_ctx_kernel,
        grid_spec=pltpu.PrefetchScalarGridSpec(
            num_scalar_prefetch=1, grid=(cb,),
            in_specs=[pl.BlockSpec((blk, ATT_QW), lambda i, s: (i, 0)),
                      pl.BlockSpec((ctx_len, kw), cx), pl.BlockSpec((ctx_len, kw), cx)],
            out_specs=pl.BlockSpec((blk, ATT_QW), lambda i, s: (i, 0))),
        out_shape=jax.ShapeDtypeStruct((ctx_len, ATT_QW), BF16),
        compiler_params=_cp(("arbitrary",)),
        name="att_context",
    )(sink, qr, k2, v2)
    return jnp.concatenate([y_ctx, y_lat], axis=0)


def _merge_kernel(a1_ref, a2_ref, a3_ref, w1_ref, w2_ref, w3_ref, g1_ref, g2_ref, g3_ref, o_ref):
    acc = _sigmoid(g1_ref[...]) * _dot(a1_ref[...], w1_ref[...].astype(BF16))
    acc = acc + _sigmoid(g2_ref[...]) * _dot(a2_ref[...], w2_ref[...].astype(BF16))
    acc = acc + _sigmoid(g3_ref[...]) * _dot(a3_ref[...], w3_ref[...].astype(BF16))
    o_ref[...] = acc.astype(BF16)


def merge_branches(y_s5, y_gla, y_att, z, w_s5, w_gla, w_att, l):
    n = z.shape[0]
    tm = _pick(n, (528, 640, 256))
    tn = 512
    a_spec = pl.BlockSpec((tm, S5_WIDTH), lambda i, j: (i, 0))
    w_spec = pl.BlockSpec((None, S5_WIDTH, tn), lambda i, j: (l, 0, j))

    def g_spec(b):
        return pl.BlockSpec((tm, tn), lambda i, j: (i, (OFF_GATE + b * D_MODEL) // tn + j))

    return pl.pallas_call(
        _merge_kernel,
        grid=(n // tm, D_MODEL // tn),
        in_specs=[a_spec, a_spec, a_spec, w_spec, w_spec, w_spec, g_spec(0), g_spec(1), g_spec(2)],
        out_specs=pl.BlockSpec((tm, tn), lambda i, j: (i, j)),
        out_shape=jax.ShapeDtypeStruct((n, D_MODEL), BF16),
        compiler_params=_cp(("arbitrary", "arbitrary")),
        name="merge",
    )(y_s5, y_gla, y_att, w_s5, w_gla, w_att, z, z, z)


def _out_kernel(m_ref, w_ref, x_ref, mod_ref, g_ref, rt_ref, x1_ref, h2_ref, lg_ref, *, ctx_len, tm):
    i = pl.program_id(0)
    is_ctx = (i * tm + lax.broadcasted_iota(jnp.int32, (tm, 1), 0)) < ctx_len
    x1 = x_ref[...] + _row_mods(mod_ref, 2, is_ctx) * _dot(m_ref[...], w_ref[...])
    x1_ref[...] = x1
    r = lax.rsqrt(jnp.mean(x1 * x1, axis=-1, keepdims=True) + EPS)
    h2 = ((x1 * r) * g_ref[0]) * (1.0 + _row_mods(mod_ref, 4, is_ctx)) + _row_mods(mod_ref, 3, is_ctx)
    h2_ref[...] = h2.astype(BF16)
    lg_ref[...] = jnp.dot(h2, rt_ref[...], preferred_element_type=F32, precision=HI)


def out_projection(m, w_out_bf, xs, mods, norm_g, router_pad, l, ctx_len):
    n = xs.shape[0]
    tm = TOK_TILE
    return pl.pallas_call(
        functools.partial(_out_kernel, ctx_len=ctx_len, tm=tm),
        grid=(n // tm,),
        in_specs=[pl.BlockSpec((tm, D_MODEL), lambda i: (i, 0)),
                  pl.BlockSpec((None, D_MODEL, D_MODEL), lambda i: (l, 0, 0)),
                  pl.BlockSpec((tm, D_MODEL), lambda i: (i, 0)),
                  pl.BlockSpec((None, 8, N_MOD * D_MODEL), lambda i: (l, 0, 0)),
                  pl.BlockSpec((1, 1, D_MODEL), lambda i: (l, 0, 0)),
                  pl.BlockSpec((None, D_MODEL, LANES), lambda i: (l, 0, 0))],
        out_specs=[pl.BlockSpec((tm, D_MODEL), lambda i: (i, 0)),
                   pl.BlockSpec((tm, D_MODEL), lambda i: (i, 0)),
                   pl.BlockSpec((tm, LANES), lambda i: (i, 0))],
        out_shape=[jax.ShapeDtypeStruct((n, D_MODEL), F32),
                   jax.ShapeDtypeStruct((n, D_MODEL), BF16),
                   jax.ShapeDtypeStruct((n, LANES), F32)],
        compiler_params=_cp(("arbitrary",)),
        name="out_projection",
    )(m, w_out_bf, xs, mods, norm_g.reshape(DEPTH, 1, D_MODEL), router_pad)


def _route_kernel(lg_ref, aff_ref, slot_ref, slot_t_ref, s0_ref, cnt_ref, *, ctx_len, n):
    tt = TOK_TILE
    lane = lax.broadcasted_iota(jnp.int32, (1, LANES), 1)
    is_expert = lane < N_EXPERTS

    def softmax_tile(b, _):
        r0 = pl.multiple_of(b * tt, tt)
        x = jnp.where(is_expert, lg_ref[pl.ds(r0, tt), :], -jnp.inf)
        e = jnp.exp(x - jnp.max(x, axis=-1, keepdims=True))
        aff_ref[pl.ds(r0, tt), :] = e / jnp.sum(e, axis=-1, keepdims=True)
        return 0

    lax.fori_loop(0, n // tt, softmax_tile, 0)

    ri = lax.broadcasted_iota(jnp.int32, (tt, tt), 0)
    ci = lax.broadcasted_iota(jnp.int32, (tt, tt), 1)
    before = jnp.where(ci < ri, 1.0, 0.0).astype(BF16)

    def bits_of(b):
        r0 = pl.multiple_of(b * tt, tt)
        return pltpu.bitcast(aff_ref[pl.ds(r0, tt), :], jnp.int32)

    def count(b0, b1, pred):
        def body(b, acc):
            return acc + jnp.sum(jnp.where(pred(bits_of(b)), 1.0, 0.0), axis=0, keepdims=True)
        return lax.fori_loop(b0, b1, body, jnp.zeros((1, LANES), F32))

    def route_set(b0, b1, slot_base):
        cap = float(EC_CAPACITY * (b1 - b0) * tt // N_EXPERTS)

        def bisect(_, lh):
            lo, hi = lh
            mid = lo + lax.shift_right_logical(hi - lo + 1, 1)
            ok = count(b0, b1, lambda v: v >= mid) >= cap
            return jnp.where(ok, mid, lo), jnp.where(ok, hi, mid - 1)

        lo0 = jnp.zeros((1, LANES), jnp.int32)
        hi0 = jnp.full((1, LANES), 0x7F800000, jnp.int32)
        thr, _ = lax.fori_loop(0, 31, bisect, (lo0, hi0))
        need = cap - count(b0, b1, lambda v: v > thr)

        def assign(b, carry):
            tie_seen, taken = carry
            r0 = pl.multiple_of(b * tt, tt)
            v = bits_of(b)
            tie = jnp.where(v == thr, 1.0, 0.0)
            tie_rank = _dot(before, tie.astype(BF16)) + tie_seen
            sel = jnp.where((v > thr) | ((v == thr) & (tie_rank < need)), 1.0, 0.0)
            rank = _dot(before, sel.astype(BF16)) + taken
            slot = jnp.where(sel > 0.0, rank + slot_base, -1.0)
            slot_ref[pl.ds(r0, tt), :] = slot.astype(jnp.int32)
            slot_t_ref[:, pl.ds(r0, tt)] = jnp.transpose(slot)[0:N_EXPERTS, :].astype(jnp.int32)
            n_sel = jnp.sum(sel, axis=0, keepdims=True)
            s0_ref[b] = (taken + slot_base).astype(jnp.int32)
            cnt_ref[b] = n_sel.astype(jnp.int32)
            return tie_seen + jnp.sum(tie, axis=0, keepdims=True), taken + n_sel

        zero = jnp.zeros((1, LANES), F32)
        lax.fori_loop(b0, b1, assign, (zero, zero))
        return cap

    cb = ctx_len // tt
    cap_ctx = route_set(0, cb, 0.0)
    route_set(cb, n // tt, cap_ctx)


def route(logits, ctx_len):
    n = logits.shape[0]
    nt = n // TOK_TILE
    full = lambda shape: pl.BlockSpec(shape, lambda i: tuple(0 for _ in shape))
    return pl.pallas_call(
        functools.partial(_route_kernel, ctx_len=ctx_len, n=n),
        grid=(1,),
        in_specs=[full((n, LANES))],
        out_specs=[full((n, LANES)), full((n, LANES)), full((N_EXPERTS, n)), full((nt, 1, LANES)),
                   full((nt, 1, LANES))],
        out_shape=[jax.ShapeDtypeStruct((n, LANES), F32),
                   jax.ShapeDtypeStruct((n, LANES), jnp.int32),
                   jax.ShapeDtypeStruct((N_EXPERTS, n), jnp.int32),
                   jax.ShapeDtypeStruct((nt, 1, LANES), jnp.int32),
                   jax.ShapeDtypeStruct((nt, 1, LANES), jnp.int32)],
        compiler_params=_cp(("arbitrary",)),
        name="route",
    )(logits)


def _window_plan(s0, cnt, rows):
    a0 = lax.shift_left(lax.shift_right_logical(s0, 3), 3)
    nwin = jnp.where(cnt > 0, lax.shift_right_logical(s0 - a0 + cnt + MOE_WIN - 1, MOE_WIN.bit_length() - 1), 0)
    return a0, nwin, rows - MOE_WIN


def _gather_kernel(s0_ref, cnt_ref, h_ref, slot_ref, xs_ref, acc_scr, *, rows):
    e = pl.program_id(0)
    t = pl.program_id(1)

    @pl.when(t == 0)
    def _():
        acc_scr[...] = jnp.zeros_like(acc_scr)

    a0, nwin, last = _window_plan(s0_ref[t, e], cnt_ref[t, e], rows)
    slot = slot_ref[0]

    def body(j, _):
        start = a0 + j * MOE_WIN
        base = pl.multiple_of(jnp.minimum(start, last), 8)
        want = base + lax.broadcasted_iota(jnp.int32, (MOE_WIN, 1), 0)
        want = jnp.where(want >= start, want, -2)
        onehot = jnp.where(slot == want, 1.0, 0.0).astype(BF16)
        acc_scr[pl.ds(base, MOE_WIN), :] += _dot(onehot, h_ref[...])
        return 0

    lax.fori_loop(0, nwin, body, 0)

    @pl.when(t == pl.num_programs(1) - 1)
    def _():
        xs_ref[0] = acc_scr[...].astype(BF16)


def moe_gather(h2, slot_t, s0, cnt, rows):
    n = h2.shape[0]
    nt = n // TOK_TILE
    return pl.pallas_call(
        functools.partial(_gather_kernel, rows=rows),
        grid_spec=pltpu.PrefetchScalarGridSpec(
            num_scalar_prefetch=2, grid=(N_EXPERTS, nt),
            in_specs=[pl.BlockSpec((TOK_TILE, D_MODEL), lambda e, t, a, b: (t, 0)),
                      pl.BlockSpec((1, 1, TOK_TILE), lambda e, t, a, b: (e, 0, t))],
            out_specs=pl.BlockSpec((1, rows, D_MODEL), lambda e, t, a, b: (e, 0, 0)),
            scratch_shapes=[pltpu.VMEM((rows, D_MODEL), F32)]),
        out_shape=jax.ShapeDtypeStruct((N_EXPERTS, rows, D_MODEL), BF16),
        compiler_params=_cp(("arbitrary", "arbitrary")),
        name="moe_gather",
    )(s0, cnt, h2, slot_t.reshape(N_EXPERTS, 1, n))


def _ffn_kernel(xs_ref, wg_ref, wu_ref, wd_ref, y_ref, acc_scr):
    f = pl.program_id(1)
    x = xs_ref[0]
    a = _dot(x, wg_ref[0].astype(BF16))
    u = _dot(x, wu_ref[0].astype(BF16))
    part = _dot(((a * _sigmoid(a)) * u).astype(BF16), wd_ref[0].astype(BF16))

    @pl.when(f == 0)
    def _():
        acc_scr[...] = part

    @pl.when(f > 0)
    def _():
        acc_scr[...] += part

    @pl.when(f == pl.num_programs(1) - 1)
    def _():
        y_ref[0] = acc_scr[...].astype(BF16)


def moe_ffn(xs, w_gate, w_up, w_down, l):
    rows = xs.shape[1]
    return pl.pallas_call(
        _ffn_kernel,
        grid=(N_EXPERTS, EXPERT_FF // FF_TILE),
        in_specs=[pl.BlockSpec((1, rows, D_MODEL), lambda e, f: (e, 0, 0)),
                  pl.BlockSpec((None, 1, D_MODEL, FF_TILE), lambda e, f: (l, e, 0, f)),
                  pl.BlockSpec((None, 1, D_MODEL, FF_TILE), lambda e, f: (l, e, 0, f)),
                  pl.BlockSpec((None, 1, FF_TILE, D_MODEL), lambda e, f: (l, e, f, 0))],
        out_specs=pl.BlockSpec((1, rows, D_MODEL), lambda e, f: (e, 0, 0)),
        out_shape=jax.ShapeDtypeStruct((N_EXPERTS, rows, D_MODEL), BF16),
        scratch_shapes=[pltpu.VMEM((rows, D_MODEL), F32)],
        compiler_params=_cp(("arbitrary", "arbitrary")),
        name="moe_ffn",
    )(xs, w_gate, w_up, w_down)


MAX_WINS = TOK_TILE // MOE_WIN + 1


def _combine_kernel(s0_ref, cnt_ref, x_ref, slot_ref, aff_ref, mod_ref, g_ref, y_hbm, o_ref, buf, sem,
                    *, rows, ctx_len, final):
    t = pl.program_id(0)
    tt = TOK_TILE

    def window_copy(e, j, a0, last):
        base = pl.multiple_of(jnp.minimum(a0 + j * MOE_WIN, last), 8)
        k = e * MAX_WINS + j
        return base, pltpu.make_async_copy(y_hbm.at[e, pl.ds(base, MOE_WIN)], buf.at[k], sem.at[k])

    for e in range(N_EXPERTS):
        a0, nwin, last = _window_plan(s0_ref[t, e], cnt_ref[t, e], rows)

        def start(j, _, e=e, a0=a0, last=last):
            window_copy(e, j, a0, last)[1].start()
            return 0

        lax.fori_loop(0, nwin, start, 0)

    o_ref[...] = jnp.zeros_like(o_ref)
    for e in range(N_EXPERTS):
        a0, nwin, last = _window_plan(s0_ref[t, e], cnt_ref[t, e], rows)

        def take(j, _, e=e, a0=a0, last=last):
            base, cp = window_copy(e, j, a0, last)
            cp.wait()
            want = base + lax.broadcasted_iota(jnp.int32, (1, MOE_WIN), 1)
            want = jnp.where(want >= a0 + j * MOE_WIN, want, -2)
            onehot = jnp.where(slot_ref[:, e:e + 1] == want, 1.0, 0.0).astype(BF16)
            o_ref[...] += aff_ref[:, e:e + 1] * _dot(onehot, buf[e * MAX_WINS + j])
            return 0

        lax.fori_loop(0, nwin, take, 0)

    is_ctx = (t * tt + lax.broadcasted_iota(jnp.int32, (tt, 1), 0)) < ctx_len
    x2 = x_ref[...] + _row_mods(mod_ref, 5, is_ctx) * o_ref[...]
    if final:
        r = lax.rsqrt(jnp.mean(x2 * x2, axis=-1, keepdims=True) + EPS)
        x2 = (x2 * r) * g_ref[...]
    o_ref[...] = x2


def moe_combine(x1, slot, aff, s0, cnt, mods, y, final_g, l, ctx_len, final):
    n = x1.shape[0]
    rows = y.shape[1]
    tt = TOK_TILE
    nbuf = N_EXPERTS * MAX_WINS
    return pl.pallas_call(
        functools.partial(_combine_kernel, rows=rows, ctx_len=ctx_len, final=final),
        grid_spec=pltpu.PrefetchScalarGridSpec(
            num_scalar_prefetch=2, grid=(n // tt,),
            in_specs=[pl.BlockSpec((tt, D_MODEL), lambda t, a, b: (t, 0)),
                      pl.BlockSpec((tt, LANES), lambda t, a, b: (t, 0)),
                      pl.BlockSpec((tt, LANES), lambda t, a, b: (t, 0)),
                      pl.BlockSpec((None, 8, N_MOD * D_MODEL), lambda t, a, b: (l, 0, 0)),
                      pl.BlockSpec((1, D_MODEL), lambda t, a, b: (0, 0)),
                      pl.BlockSpec(memory_space=pl.ANY)],
            out_specs=pl.BlockSpec((tt, D_MODEL), lambda t, a, b: (t, 0)),
            scratch_shapes=[pltpu.VMEM((nbuf, MOE_WIN, D_MODEL), BF16),
                            pltpu.SemaphoreType.DMA((nbuf,))]),
        out_shape=jax.ShapeDtypeStruct((n, D_MODEL), F32),
        compiler_params=_cp(("arbitrary",)),
        name="moe_combine",
    )(s0, cnt, x1, slot, aff, mods, final_g.reshape(1, D_MODEL), y)


def rope_tables(seq, ctx_len):
    pos = jnp.arange(seq)
    row = (pos // GRID_W).astype(F32)
    col = (pos % GRID_W).astype(F32)
    n_freq = ATT_HEAD_DIM // 4
    inv_freq = ROPE_BASE ** (-jnp.arange(n_freq, dtype=F32) / n_freq)
    ang = jnp.concatenate([row[:, None] * inv_freq, col[:, None] * inv_freq], axis=-1)
    cos, sin = jnp.cos(ang), jnp.sin(ang)
    cos_t = jnp.concatenate([jnp.ones((ctx_len, LANES), F32), jnp.tile(cos, (1, 4))], axis=0)
    sin_t = jnp.concatenate([jnp.zeros((ctx_len, LANES), F32), jnp.tile(jnp.concatenate([-sin, sin], -1), (1, 2))],
                            axis=0)
    return cos_t, sin_t


def kernel(x, c, ctx, c_ctx, ada_w, ada_b, norm1_g, norm2_g, w_in, s5_lam_re, s5_lam_im, s5_log_dt,
           s5_b_re, s5_b_im, s5_c_re, s5_c_im, s5_d, s5_w_glu, gla_w1, gla_w2, gla_b, gla_norm_g,
           attn_sink, w_branch_s5, w_branch_gla, w_branch_attn, w_out, moe_router, moe_w_gate,
           moe_w_up, moe_w_down, final_g):
    seq, ctx_len = x.shape[1], ctx.shape[1]
    n = seq + ctx_len
    assert x.shape[0] == 1 and ctx_len % TOK_TILE == 0 and seq % TOK_TILE == 0
    xs = jnp.concatenate([ctx[0], x[0]], axis=0)
    c8 = jnp.zeros((8, D_MODEL), F32).at[0].set(c[0]).at[1].set(c_ctx)
    mods = ada_mods(c8, ada_w, ada_b)
    cos_t, sin_t = rope_tables(seq, ctx_len)
    w_out_bf = w_out.astype(BF16)
    router_pad = jnp.pad(moe_router, ((0, 0), (0, 0), (0, LANES - N_EXPERTS)))
    rows = EC_CAPACITY * n // N_EXPERTS
    for l in range(DEPTH):
        w1cat = jnp.pad(jnp.concatenate([gla_w1[l, 0], gla_w1[l, 1]], axis=-1),
                        ((0, 0), (0, LANES - 2 * GLA_RANK)))
        w2pad = jnp.zeros((2, LANES, GLA_KW), F32)
        w2pad = w2pad.at[0, 0:GLA_RANK].set(gla_w2[l, 0]).at[1, GLA_RANK:2 * GLA_RANK].set(gla_w2[l, 1])
        z, lr = in_projection(xs, norm1_g, mods, w_in, w1cat, l, ctx_len)
        s5w = s5_weights(s5_lam_re[l], s5_lam_im[l], s5_log_dt[l], s5_b_re[l], s5_b_im[l], s5_c_re[l], s5_c_im[l])
        y_s5 = s5_glu(s5_mix(z[:, OFF_U:OFF_U + S5_WIDTH], s5w, s5_d[l], ctx_len), s5_w_glu, l)
        gla_bias = gla_b[l].reshape(2, 1, GLA_KW)
        o_f = gla_direction(z, lr, w2pad, gla_bias, 0, ctx_len)
        o_b = gla_direction(z, lr, w2pad, gla_bias, 1, ctx_len)
        y_gla = gla_post(o_f, o_b, z, gla_norm_g, l)
        qr, k2, v2 = att_prep(z, cos_t, sin_t)
        y_att = attention(qr, k2, v2, attn_sink[l], ctx_len)
        m = merge_branches(y_s5, y_gla, y_att, z, w_branch_s5, w_branch_gla, w_branch_attn, l)
        x1, h2, logits = out_projection(m, w_out_bf, xs, mods, norm2_g, router_pad, l, ctx_len)
        aff, slot, slot_t, s0, cnt = route(logits, ctx_len)
        s0, cnt = s0[:, 0, :N_EXPERTS], cnt[:, 0, :N_EXPERTS]
        xe = moe_gather(h2, slot_t, s0, cnt, rows)
        ye = moe_ffn(xe, moe_w_gate, moe_w_up, moe_w_down, l)
        xs = moe_combine(x1, slot, aff, s0, cnt, mods, ye, final_g, l, ctx_len, l == DEPTH - 1)
    return xs[ctx_len:][None]
```

```python
import functools
import math

import jax
import jax.numpy as jnp
from jax import lax
from jax.experimental import pallas as pl
from jax.experimental.pallas import tpu as pltpu

D_MODEL = 2048
DEPTH = 2
GRID_W = 64
EPS = 1e-6
N_MOD = 6
S5_WIDTH = 1024
S5_GROUP_CH = 16
S5_GROUPS = S5_WIDTH // S5_GROUP_CH
S5_STATE = 64
GLA_HEADS = 4
GLA_DK = 128
GLA_DV = 256
GLA_KW = GLA_HEADS * GLA_DK
GLA_VW = GLA_HEADS * GLA_DV
GLA_RANK = 16
GLA_TAU = 16.0
GLA_CHUNK = 64
ATT_HEADS = 16
ATT_KV_HEADS = 4
ATT_HEAD_DIM = 64
ATT_QW = ATT_HEADS * ATT_HEAD_DIM
ATT_KVW = ATT_KV_HEADS * ATT_HEAD_DIM
WINDOW = 128
ATT_BLOCK = 128
ROPE_BASE = 10000.0
N_BRANCH = 3
IN_SPLITS = (S5_WIDTH, GLA_KW, GLA_KW, GLA_VW, GLA_VW, ATT_QW, ATT_KVW, ATT_KVW, N_BRANCH * D_MODEL)
IN_WIDTH = sum(IN_SPLITS)
N_EXPERTS = 16
EXPERT_FF = 2048
EC_CAPACITY = 2

OFF_U = 0
OFF_GQ = OFF_U + S5_WIDTH
OFF_GK = OFF_GQ + GLA_KW
OFF_GV = OFF_GK + GLA_KW
OFF_GR = OFF_GV + GLA_VW
OFF_AQ = OFF_GR + GLA_VW
OFF_AK = OFF_AQ + ATT_QW
OFF_AV = OFF_AK + ATT_KVW
OFF_GATE = OFF_AV + ATT_KVW

LANES = 128
S5_T = 16
S5_PAIRS = S5_GROUPS // 2
S5_BLOCKS = S5_WIDTH // LANES
S5_BLK_GROUPS = LANES // S5_GROUP_CH
S5_BLK_PAIRS = S5_BLK_GROUPS // 2
S5_BLK_STATE = S5_BLK_GROUPS * S5_STATE
TOK_TILE = 256
MOE_WIN = 64
FF_TILE = 256
VMEM_MB = 56

F32 = jnp.float32
BF16 = jnp.bfloat16
HI = lax.Precision.HIGHEST


def _cp(sem, mb=VMEM_MB):
    return pltpu.CompilerParams(dimension_semantics=sem, vmem_limit_bytes=mb << 20)


def _dot(a, b):
    return jnp.dot(a, b, preferred_element_type=F32)


def _dot_nt(a, b):
    return lax.dot_general(a, b, (((1,), (1,)), ((), ())), preferred_element_type=F32)


def _dot_tn(a, b):
    return lax.dot_general(a, b, (((0,), (0,)), ((), ())), preferred_element_type=F32)


def _pick(n, cands):
    for c in cands:
        if n % c == 0:
            return c
    raise ValueError(f"no tile for {n} in {cands}")


def _sigmoid(x):
    return 1.0 / (1.0 + jnp.exp(-x))


def _row_mods(mod_ref, k, is_ctx):
    lo, hi = k * D_MODEL, (k + 1) * D_MODEL
    return jnp.where(is_ctx, mod_ref[1:2, lo:hi], mod_ref[0:1, lo:hi])


def _ada_kernel(c_ref, w_ref, b_ref, o_ref):
    cc = c_ref[...]
    o_ref[0] = jnp.dot(cc * _sigmoid(cc), w_ref[0], preferred_element_type=F32, precision=HI) + b_ref[0]


def ada_mods(c8, ada_w, ada_b):
    tn = 1024
    width = N_MOD * D_MODEL
    return pl.pallas_call(
        _ada_kernel,
        grid=(DEPTH, width // tn),
        in_specs=[pl.BlockSpec((8, D_MODEL), lambda l, j: (0, 0)),
                  pl.BlockSpec((1, D_MODEL, tn), lambda l, j: (l, 0, j)),
                  pl.BlockSpec((1, 1, tn), lambda l, j: (l, 0, j))],
        out_specs=pl.BlockSpec((1, 8, tn), lambda l, j: (l, 0, j)),
        out_shape=jax.ShapeDtypeStruct((DEPTH, 8, width), F32),
        compiler_params=_cp(("arbitrary", "arbitrary")),
        name="ada_mods",
    )(c8, ada_w, ada_b.reshape(DEPTH, 1, width))


def _in_kernel(x_ref, g_ref, mod_ref, w_ref, w1_ref, z_ref, lr_ref, h_scr, *, ctx_len, tm):
    i = pl.program_id(0)
    j = pl.program_id(1)

    @pl.when(j == 0)
    def _():
        x = x_ref[...]
        r = lax.rsqrt(jnp.mean(x * x, axis=-1, keepdims=True) + EPS)
        hn = (x * r) * g_ref[0]
        is_ctx = (i * tm + lax.broadcasted_iota(jnp.int32, (tm, 1), 0)) < ctx_len
        hb = (hn * (1.0 + _row_mods(mod_ref, 1, is_ctx)) + _row_mods(mod_ref, 0, is_ctx)).astype(BF16)
        h_scr[...] = hb
        lr_ref[...] = _dot(hb, w1_ref[...].astype(BF16))

    z_ref[...] = _dot(h_scr[...], w_ref[...].astype(BF16))


def in_projection(xs, norm_g, mods, w_in, w1cat, l, ctx_len):
    n = xs.shape[0]
    tm = _pick(n, (1056, 640, 256))
    tn = 512
    return pl.pallas_call(
        functools.partial(_in_kernel, ctx_len=ctx_len, tm=tm),
        grid=(n // tm, IN_WIDTH // tn),
        in_specs=[pl.BlockSpec((tm, D_MODEL), lambda i, j: (i, 0)),
                  pl.BlockSpec((1, 1, D_MODEL), lambda i, j: (l, 0, 0)),
                  pl.BlockSpec((None, 8, N_MOD * D_MODEL), lambda i, j: (l, 0, 0)),
                  pl.BlockSpec((None, D_MODEL, tn), lambda i, j: (l, 0, j)),
                  pl.BlockSpec((D_MODEL, LANES), lambda i, j: (0, 0))],
        out_specs=[pl.BlockSpec((tm, tn), lambda i, j: (i, j)),
                   pl.BlockSpec((tm, LANES), lambda i, j: (i, 0))],
        out_shape=[jax.ShapeDtypeStruct((n, IN_WIDTH), F32),
                   jax.ShapeDtypeStruct((n, LANES), F32)],
        scratch_shapes=[pltpu.VMEM((tm, D_MODEL), BF16)],
        compiler_params=_cp(("arbitrary", "arbitrary")),
        name="in_projection",
    )(xs, norm_g.reshape(DEPTH, 1, D_MODEL), mods, w_in, w1cat)


def _s5_direction_weights(lam_re, lam_im, log_dt, b_re, b_im, c_re, c_im, rev):
    t_len, g_n, n_st, ch = S5_T, S5_GROUPS, S5_STATE, S5_GROUP_CH
    dt = jnp.exp(log_dt)[:, None]
    mag = jnp.exp(lam_re * dt)
    ab_re = mag * jnp.cos(lam_im * dt)
    ab_im = mag * jnp.sin(lam_im * dt)
    den = lam_re * lam_re + lam_im * lam_im
    nr = ab_re - 1
    f_re = ((nr * lam_re + ab_im * lam_im) / den)[..., None]
    f_im = ((ab_im * lam_re - nr * lam_im) / den)[..., None]
    bb_re = f_re * b_re - f_im * b_im
    bb_im = f_re * b_im + f_im * b_re
    tau = jnp.arange(t_len + 1, dtype=F32)[:, None, None]
    pmag = jnp.exp(lam_re * dt * tau)
    ang = lam_im * dt * tau
    p_re, p_im = pmag * jnp.cos(ang), pmag * jnp.sin(ang)

    def c_times(pr, pi):
        return (c_re[None] * pr[:, :, None, :] - c_im[None] * pi[:, :, None, :],
                c_re[None] * pi[:, :, None, :] + c_im[None] * pr[:, :, None, :])

    cp_re, cp_im = c_times(p_re[:t_len], p_im[:t_len])
    kern = (jnp.einsum('tgon,gni->tgoi', cp_re, bb_re, precision=HI)
            - jnp.einsum('tgon,gni->tgoi', cp_im, bb_im, precision=HI))
    eye_g = jnp.eye(S5_BLK_GROUPS, dtype=F32)
    eye2 = jnp.eye(2, dtype=F32)
    pw = 2 * ch

    kt = kern.transpose(0, 1, 3, 2).reshape(t_len, S5_BLOCKS, S5_BLK_GROUPS, ch, ch)
    kq = jnp.einsum('tqgio,gh->tqgiho', kt, eye_g).reshape(t_len, S5_BLOCKS, LANES, LANES)
    w_toep = (kq if rev else kq[::-1]).transpose(1, 0, 2, 3).reshape(S5_BLOCKS, t_len * LANES, LANES)

    def pair_in(w):
        w5 = w.transpose(0, 1, 3, 2).reshape(t_len, S5_PAIRS, 2, ch, n_st)
        return jnp.einsum('tpgcn,gh->tpgchn', w5, eye2).reshape(t_len, S5_PAIRS, pw, 2 * n_st)

    e_idx = jnp.arange(t_len) if rev else t_len - 1 - jnp.arange(t_len)
    pe_re, pe_im = p_re[e_idx][..., None], p_im[e_idx][..., None]
    wb = jnp.concatenate([pair_in(pe_re * bb_re[None] - pe_im * bb_im[None]),
                          pair_in(pe_re * bb_im[None] + pe_im * bb_re[None])], axis=-1)
    wb = wb.reshape(t_len, S5_BLOCKS, S5_BLK_PAIRS, pw, 2 * LANES).transpose(1, 0, 2, 3, 4)

    def pair_out(w):
        w5 = w.transpose(1, 3, 0, 2).reshape(S5_PAIRS, 2, n_st, t_len, ch)
        return jnp.einsum('pgntc,gh->pgnthc', w5, eye2).reshape(S5_PAIRS, 2 * n_st, t_len * pw)

    f_idx = (t_len - jnp.arange(t_len)) if rev else (jnp.arange(t_len) + 1)
    e_re, e_im = c_times(p_re[f_idx], p_im[f_idx])
    wc = jnp.concatenate([pair_out(e_re), pair_out(-e_im)], axis=1)
    wc = wc.reshape(S5_BLOCKS, S5_BLK_PAIRS, 2 * LANES, t_len * pw)
    a_re = p_re[t_len].reshape(S5_BLOCKS, S5_BLK_STATE)
    a_im = p_im[t_len].reshape(S5_BLOCKS, S5_BLK_STATE)
    return w_toep.astype(BF16), wb.astype(BF16), wc, a_re, a_im


def s5_weights(lam_re, lam_im, log_dt, b_re, b_im, c_re, c_im):
    per_dir = [_s5_direction_weights(lam_re[d], lam_im[d], log_dt[d], b_re[d], b_im[d], c_re[d], c_im[d], d == 1)
               for d in range(2)]
    return tuple(jnp.stack(t) for t in zip(*per_dir))


def _s5_chunk_rows(z_ref, nch):
    return jnp.concatenate([z_ref[pl.ds(t, nch, stride=S5_T), :].astype(BF16) for t in range(S5_T)], axis=-1)


def _s5_state_in_kernel(z_ref, wb_ref, vre_ref, vim_ref, wq_scr, *, nch):
    @pl.when((pl.program_id(0) == 0) & (pl.program_id(1) == 0))
    def _():
        wq_scr[...] = jnp.zeros_like(wq_scr)

    pw = 2 * S5_GROUP_CH
    for t in range(S5_T):
        for p in range(S5_BLK_PAIRS):
            r0 = t * LANES + p * pw
            blk = wb_ref[0, 0, t, p]
            wq_scr[r0:r0 + pw, p * LANES:(p + 1) * LANES] = blk[:, :LANES]
            wq_scr[r0:r0 + pw, S5_BLK_STATE + p * LANES:S5_BLK_STATE + (p + 1) * LANES] = blk[:, LANES:]
    v = _dot(_s5_chunk_rows(z_ref, nch), wq_scr[...])
    vre_ref[0] = v[:, :S5_BLK_STATE]
    vim_ref[0] = v[:, S5_BLK_STATE:]


def _s5_scan_kernel(vre_ref, vim_ref, are_ref, aim_ref, sre_ref, sim_ref, *, nch, ncc):
    rev = pl.program_id(0) == 1
    ar = are_ref[0]
    ai = aim_ref[0]

    def body(i, carry):
        sr, si = carry
        k = jnp.where(rev, jnp.where(i < ncc, ncc - 1 - i, nch + ncc - 1 - i), i)
        sre_ref[0, k] = sr
        sim_ref[0, k] = si
        return ar * sr - ai * si + vre_ref[0, k], ar * si + ai * sr + vim_ref[0, k]

    zero = jnp.zeros((S5_BLOCKS, LANES), F32)
    lax.fori_loop(0, nch, body, (zero, zero))


def _s5_out_kernel(z_ref, sre_ref, sim_ref, wt_ref, wc_ref, d_ref, y_ref, wq_scr, *, nch):
    lane = lax.broadcasted_iota(jnp.int32, (1, LANES), 1)
    pw = 2 * S5_GROUP_CH
    for d in range(2):
        for p in range(S5_BLK_PAIRS):
            keep = (lane >= p * pw) & (lane < (p + 1) * pw)
            for part in range(2):
                r0 = part * S5_BLK_STATE + p * LANES
                for t in range(S5_T):
                    c0 = (t * pw // LANES) * LANES
                    x = wc_ref[d, 0, p, part * LANES:(part + 1) * LANES, c0:c0 + LANES]
                    shift = (p * pw - (t * pw - c0)) % LANES
                    if shift:
                        x = pltpu.roll(x, shift, 1)
                    wq_scr[d, r0:r0 + LANES, t * LANES:(t + 1) * LANES] = jnp.where(keep, x, 0.0).astype(BF16)
    rows = _s5_chunk_rows(z_ref, nch)
    sp = [jnp.concatenate([sre_ref[d], sim_ref[d]], axis=-1).astype(BF16) for d in range(2)]
    for t in range(S5_T):
        acc = z_ref[pl.ds(t, nch, stride=S5_T), :] * d_ref[...]
        acc = acc + _dot(rows[:, :LANES * (t + 1)], wt_ref[0, 0, LANES * (S5_T - 1 - t):, :])
        acc = acc + _dot(rows[:, LANES * t:], wt_ref[1, 0, :LANES * (S5_T - t), :])
        for d in range(2):
            acc = acc + _dot(sp[d], wq_scr[d, :, t * LANES:(t + 1) * LANES])
        y_ref[pl.ds(t, nch, stride=S5_T), :] = acc


def s5_mix(z, weights, d_skip, ctx_len):
    n = z.shape[0]
    nch, ncc = n // S5_T, ctx_len // S5_T
    w_toep, wb, wc, a_re, a_im = weights
    st, tl = S5_BLK_STATE, S5_T * LANES
    v_shape = jax.ShapeDtypeStruct((2, nch, S5_BLOCKS * st), F32)
    v_re, v_im = pl.pallas_call(
        functools.partial(_s5_state_in_kernel, nch=nch),
        grid=(S5_BLOCKS, 2),
        in_specs=[pl.BlockSpec((n, LANES), lambda q, d: (0, q)),
                  pl.BlockSpec((1, 1, S5_T, S5_BLK_PAIRS, 2 * S5_GROUP_CH, 2 * LANES),
                               lambda q, d: (d, q, 0, 0, 0, 0))],
        out_specs=[pl.BlockSpec((1, nch, st), lambda q, d: (d, 0, q))] * 2,
        out_shape=[v_shape, v_shape],
        scratch_shapes=[pltpu.VMEM((tl, 2 * st), BF16)],
        compiler_params=_cp(("arbitrary", "arbitrary")),
        name="s5_state_in",
    )(z, wb)
    blocked = (2, nch, S5_BLOCKS, st)
    v_spec = pl.BlockSpec((1, nch, S5_BLOCKS, LANES), lambda d, j: (d, 0, 0, j))
    a_spec = pl.BlockSpec((1, S5_BLOCKS, LANES), lambda d, j: (d, 0, j))
    s_shape = jax.ShapeDtypeStruct(blocked, F32)
    s_re, s_im = pl.pallas_call(
        functools.partial(_s5_scan_kernel, nch=nch, ncc=ncc),
        grid=(2, st // LANES),
        in_specs=[v_spec, v_spec, a_spec, a_spec],
        out_specs=[v_spec, v_spec],
        out_shape=[s_shape, s_shape],
        compiler_params=_cp(("arbitrary", "arbitrary")),
        name="s5_scan",
    )(v_re.reshape(blocked), v_im.reshape(blocked), a_re, a_im)
    s_spec = pl.BlockSpec((2, nch, st), lambda q: (0, 0, q))
    return pl.pallas_call(
        functools.partial(_s5_out_kernel, nch=nch),
        grid=(S5_BLOCKS,),
        in_specs=[pl.BlockSpec((n, LANES), lambda q: (0, q)), s_spec, s_spec,
                  pl.BlockSpec((2, 1, tl, LANES), lambda q: (0, q, 0, 0)),
                  pl.BlockSpec((2, 1, S5_BLK_PAIRS, 2 * LANES, S5_T * 2 * S5_GROUP_CH), lambda q: (0, q, 0, 0, 0)),
                  pl.BlockSpec((1, LANES), lambda q: (0, q))],
        out_specs=pl.BlockSpec((n, LANES), lambda q: (0, q)),
        out_shape=jax.ShapeDtypeStruct((n, S5_WIDTH), F32),
        scratch_shapes=[pltpu.VMEM((2, 2 * st, tl), BF16)],
        compiler_params=_cp(("arbitrary",)),
        name="s5_out",
    )(z, s_re.reshape(2, nch, S5_BLOCKS * st), s_im.reshape(2, nch, S5_BLOCKS * st), w_toep, wc,
      d_skip.reshape(1, S5_WIDTH))


def _s5_glu_kernel(y_ref, w_ref, o_ref):
    y = y_ref[...]
    g = 0.5 * y * (1.0 + jnp.tanh(math.sqrt(2.0 / math.pi) * (y + 0.044715 * (y * y * y))))
    o_ref[...] = (g * _sigmoid(_dot(g.astype(BF16), w_ref[...].astype(BF16)))).astype(BF16)


def s5_glu(y, w_glu, l):
    n = y.shape[0]
    tm = _pick(n, (1056, 640, 256))
    return pl.pallas_call(
        _s5_glu_kernel,
        grid=(n // tm,),
        in_specs=[pl.BlockSpec((tm, S5_WIDTH), lambda i: (i, 0)),
                  pl.BlockSpec((None, S5_WIDTH, S5_WIDTH), lambda i: (l, 0, 0))],
        out_specs=pl.BlockSpec((tm, S5_WIDTH), lambda i: (i, 0)),
        out_shape=jax.ShapeDtypeStruct((n, S5_WIDTH), BF16),
        compiler_params=_cp(("arbitrary",)),
        name="s5_glu",
    )(y, w_glu)


def _gla_kernel(q_ref, k_ref, v_ref, lr_ref, w2_ref, b_ref, o_ref, st_scr, *, rev):
    c = GLA_CHUNK

    @pl.when(pl.program_id(0) == 0)
    def _():
        st_scr[...] = jnp.zeros_like(st_scr)

    x = _dot(lr_ref[...].astype(BF16), w2_ref[0].astype(BF16)) + b_ref[0]
    la = (jnp.minimum(x, 0.0) - jnp.log1p(jnp.exp(-jnp.abs(x)))) * (1.0 / GLA_TAU)
    ri = lax.broadcasted_iota(jnp.int32, (c, c), 0)
    ci = lax.broadcasted_iota(jnp.int32, (c, c), 1)
    keep = (ci >= ri) if rev else (ci <= ri)
    tri = jnp.where(keep, 1.0, 0.0).astype(BF16)
    hi = la.astype(BF16)
    r1 = la - hi.astype(F32)
    mid = r1.astype(BF16)
    lo = (r1 - mid.astype(F32)).astype(BF16)
    bc = _dot(tri, hi) + _dot(tri, mid) + _dot(tri, lo)
    last, ref_row = (0, c // 2) if rev else (c - 1, c // 2 - 1)
    btot = bc[last:last + 1]
    bmid = bc[ref_row:ref_row + 1]
    qs = q_ref[...] * (GLA_DK ** -0.5)
    kk = k_ref[...]
    q_mid = (qs * jnp.exp(bc - bmid)).astype(BF16)
    k_mid = (kk * jnp.exp(bmid - bc)).astype(BF16)
    q_in = (qs * jnp.exp(bc)).astype(BF16)
    k_out = (kk * jnp.exp(btot - bc)).astype(BF16)
    decay = jnp.exp(btot)
    for h in range(GLA_HEADS):
        ks = slice(h * GLA_DK, (h + 1) * GLA_DK)
        vs = slice(h * GLA_DV, (h + 1) * GLA_DV)
        vb = v_ref[:, vs].astype(BF16)
        sc = jnp.where(keep, _dot_nt(q_mid[:, ks], k_mid[:, ks]), 0.0).astype(BF16)
        st = st_scr[h]
        o_ref[:, vs] = _dot(sc, vb) + _dot_nt(q_in[:, ks], st.astype(BF16))
        st_scr[h] = decay[:, ks] * st + _dot_tn(vb, k_out[:, ks])


def gla_direction(z, lr, w2pad, bias, d, ctx_len):
    n = z.shape[0]
    c = GLA_CHUNK
    nch, ncc = n // c, ctx_len // c
    if d == 0:
        row = lambda i: i
    else:
        row = lambda i: jnp.where(i < ncc, ncc - 1 - i, nch + ncc - 1 - i)
    return pl.pallas_call(
        functools.partial(_gla_kernel, rev=d == 1),
        grid=(nch,),
        in_specs=[pl.BlockSpec((c, GLA_KW), lambda i: (row(i), OFF_GQ // GLA_KW)),
                  pl.BlockSpec((c, GLA_KW), lambda i: (row(i), OFF_GK // GLA_KW)),
                  pl.BlockSpec((c, GLA_VW), lambda i: (row(i), OFF_GV // GLA_VW)),
                  pl.BlockSpec((c, LANES), lambda i: (row(i), 0)),
                  pl.BlockSpec((1, LANES, GLA_KW), lambda i: (d, 0, 0)),
                  pl.BlockSpec((1, 1, GLA_KW), lambda i: (d, 0, 0))],
        out_specs=pl.BlockSpec((c, GLA_VW), lambda i: (row(i), 0)),
        out_shape=jax.ShapeDtypeStruct((n, GLA_VW), F32),
        scratch_shapes=[pltpu.VMEM((GLA_HEADS, GLA_DV, GLA_DK), F32)],
        compiler_params=_cp(("arbitrary",)),
        name=f"gla_dir{d}",
    )(z, z, z, lr, w2pad, bias)


def _gla_post_kernel(of_ref, ob_ref, r_ref, g_ref, y_ref):
    for h in range(GLA_HEADS):
        vs = slice(h * GLA_DV, (h + 1) * GLA_DV)
        o = of_ref[:, vs] + ob_ref[:, vs]
        rn = lax.rsqrt(jnp.mean(o * o, axis=-1, keepdims=True) + EPS)
        r = r_ref[:, vs]
        y_ref[:, vs] = ((o * rn) * g_ref[0, :, vs] * (r * _sigmoid(r))).astype(BF16)


def gla_post(o_f, o_b, z, norm_g, l):
    n = z.shape[0]
    tm = _pick(n, (1056, 640, 256))
    return pl.pallas_call(
        _gla_post_kernel,
        grid=(n // tm,),
        in_specs=[pl.BlockSpec((tm, GLA_VW), lambda i: (i, 0)),
                  pl.BlockSpec((tm, GLA_VW), lambda i: (i, 0)),
                  pl.BlockSpec((tm, GLA_VW), lambda i: (i, OFF_GR // GLA_VW)),
                  pl.BlockSpec((1, 1, GLA_VW), lambda i: (l, 0, 0))],
        out_specs=pl.BlockSpec((tm, GLA_VW), lambda i: (i, 0)),
        out_shape=jax.ShapeDtypeStruct((n, GLA_VW), BF16),
        compiler_params=_cp(("arbitrary",)),
        name="gla_post",
    )(o_f, o_b, z, norm_g.reshape(DEPTH, 1, GLA_VW))


def _att_prep_kernel(q_ref, k_ref, v_ref, cos_ref, sin_ref, qo_ref, ko_ref, vo_ref):
    cos = cos_ref[...]
    sin = sin_ref[...]
    lane = lax.broadcasted_iota(jnp.int32, (1, LANES), 1)
    first_half = (lane % ATT_HEAD_DIM) < (ATT_HEAD_DIM // 2)
    low_head = lane < ATT_HEAD_DIM

    def rope(xc):
        rot = jnp.where(first_half, pltpu.roll(xc, LANES - ATT_HEAD_DIM // 2, 1), pltpu.roll(xc, ATT_HEAD_DIM // 2, 1))
        return xc * cos + rot * sin

    def dup(xc, parity):
        sw = pltpu.roll(xc, ATT_HEAD_DIM, 1)
        return jnp.where(low_head, xc, sw) if parity == 0 else jnp.where(low_head, sw, xc)

    for cidx in range(ATT_QW // LANES):
        cs = slice(cidx * LANES, (cidx + 1) * LANES)
        qo_ref[:, cs] = (rope(q_ref[:, cs]) * (ATT_HEAD_DIM ** -0.5)).astype(BF16)
    for kh in range(ATT_KV_HEADS):
        ps = slice((kh // 2) * LANES, (kh // 2 + 1) * LANES)
        os_ = slice(kh * LANES, (kh + 1) * LANES)
        ko_ref[:, os_] = dup(rope(k_ref[:, ps]), kh % 2).astype(BF16)
        vo_ref[:, os_] = dup(v_ref[:, ps], kh % 2).astype(BF16)


def att_prep(z, cos_t, sin_t):
    n = z.shape[0]
    tm = 256
    kw = ATT_KV_HEADS * LANES
    return pl.pallas_call(
        _att_prep_kernel,
        grid=(n // tm,),
        in_specs=[pl.BlockSpec((tm, ATT_QW), lambda i: (i, OFF_AQ // ATT_QW)),
                  pl.BlockSpec((tm, ATT_KVW), lambda i: (i, OFF_AK // ATT_KVW)),
                  pl.BlockSpec((tm, ATT_KVW), lambda i: (i, OFF_AV // ATT_KVW)),
                  pl.BlockSpec((tm, LANES), lambda i: (i, 0)),
                  pl.BlockSpec((tm, LANES), lambda i: (i, 0))],
        out_specs=[pl.BlockSpec((tm, ATT_QW), lambda i: (i, 0)),
                   pl.BlockSpec((tm, kw), lambda i: (i, 0)),
                   pl.BlockSpec((tm, kw), lambda i: (i, 0))],
        out_shape=[jax.ShapeDtypeStruct((n, ATT_QW), BF16),
                   jax.ShapeDtypeStruct((n, kw), BF16),
                   jax.ShapeDtypeStruct((n, kw), BF16)],
        compiler_params=_cp(("arbitrary",)),
        name="att_prep",
    )(z, z, z, cos_t, sin_t)


def _att_heads(q_ref, k_all, v_all, bias, sink_ref, o_ref):
    lane = lax.broadcasted_iota(jnp.int32, (1, LANES), 1)
    low_head = lane < ATT_HEAD_DIM
    grp = ATT_HEADS // ATT_KV_HEADS
    blk = q_ref.shape[0]
    for kh in range(ATT_KV_HEADS):
        k2 = k_all[:, kh * LANES:(kh + 1) * LANES]
        v2 = v_all[:, kh * LANES:(kh + 1) * LANES]
        qs, sinks = [], []
        for j in range(grp):
            h = kh * grp + j
            qp = q_ref[:, (h // 2) * LANES:(h // 2 + 1) * LANES]
            qs.append(jnp.where(low_head if h % 2 == 0 else jnp.logical_not(low_head), qp, jnp.zeros_like(qp)))
            sinks.append(jnp.full((blk, 1), sink_ref[h], F32))
        s = _dot_nt(jnp.concatenate(qs, axis=0), k2)
        if bias is not None:
            s = s + jnp.concatenate([bias] * grp, axis=0)
        sk = jnp.concatenate(sinks, axis=0)
        m = jnp.maximum(jnp.max(s, axis=-1, keepdims=True), sk)
        p = jnp.exp(s - m)
        den = jnp.sum(p, axis=-1, keepdims=True) + jnp.exp(sk - m)
        o = _dot(p.astype(BF16), v2) * (1.0 / den)
        for pr in range(grp // 2):
            pair = kh * (grp // 2) + pr
            even, odd = o[(2 * pr) * blk:(2 * pr + 1) * blk], o[(2 * pr + 1) * blk:(2 * pr + 2) * blk]
            o_ref[:, pair * LANES:(pair + 1) * LANES] = jnp.where(low_head, even, odd).astype(BF16)


def _att_kernel(sink_ref, q_ref, kp_ref, kc_ref, kn_ref, kx_ref, vp_ref, vc_ref, vn_ref, vx_ref, o_ref, *, seq):
    i = pl.program_id(0)
    blk = ATT_BLOCK
    k_all = jnp.concatenate([kp_ref[...], kc_ref[...], kn_ref[...], kx_ref[...]], axis=0)
    v_all = jnp.concatenate([vp_ref[...], vc_ref[...], vn_ref[...], vx_ref[...]], axis=0)
    nctx = kx_ref.shape[0]
    qpos = i * blk + lax.broadcasted_iota(jnp.int32, (blk, 1), 0)
    kpos = (i - 1) * blk + lax.broadcasted_iota(jnp.int32, (1, 3 * blk), 1)
    valid = (jnp.abs(kpos - qpos) <= WINDOW) & (kpos >= 0) & (kpos < seq)
    bias = jnp.concatenate([jnp.where(valid, 0.0, -jnp.inf), jnp.zeros((blk, nctx), F32)], axis=1)
    _att_heads(q_ref, k_all, v_all, bias, sink_ref, o_ref)


def _att_ctx_kernel(sink_ref, q_ref, kx_ref, vx_ref, o_ref):
    _att_heads(q_ref, kx_ref[...], vx_ref[...], None, sink_ref, o_ref)


def attention(qr, k2, v2, sink, ctx_len):
    n = qr.shape[0]
    blk = ATT_BLOCK
    seq = n - ctx_len
    nb, cb = seq // blk, ctx_len // blk
    kw = ATT_KV_HEADS * LANES
    prev = lambda i, s: (jnp.maximum(i - 1, 0) + cb, 0)
    cur = lambda i, s: (i + cb, 0)
    nxt = lambda i, s: (jnp.minimum(i + 1, nb - 1) + cb, 0)
    cx = lambda i, s: (0, 0)
    band = [pl.BlockSpec((blk, kw), prev), pl.BlockSpec((blk, kw), cur), pl.BlockSpec((blk, kw), nxt),
            pl.BlockSpec((ctx_len, kw), cx)]
    y_lat = pl.pallas_call(
        functools.partial(_att_kernel, seq=seq),
        grid_spec=pltpu.PrefetchScalarGridSpec(
            num_scalar_prefetch=1, grid=(nb,),
            in_specs=[pl.BlockSpec((blk, ATT_QW), cur)] + band + band,
            out_specs=pl.BlockSpec((blk, ATT_QW), lambda i, s: (i, 0))),
        out_shape=jax.ShapeDtypeStruct((seq, ATT_QW), BF16),
        compiler_params=_cp(("arbitrary",)),
        name="att_window",
    )(sink, qr, k2, k2, k2, k2, v2, v2, v2, v2)
    y_ctx = pl.pallas_call(
        _att_ctx_kernel,
        grid_spec=pltpu.PrefetchScalarGridSpec(
            num_scalar_prefetch=1, grid=(cb,),
            in_specs=[pl.BlockSpec((blk, ATT_QW), lambda i, s: (i, 0)),
                      pl.BlockSpec((ctx_len, kw), cx), pl.BlockSpec((ctx_len, kw), cx)],
            out_specs=pl.BlockSpec((blk, ATT_QW), lambda i, s: (i, 0))),
        out_shape=jax.ShapeDtypeStruct((ctx_len, ATT_QW), BF16),
        compiler_params=_cp(("arbitrary",)),
        name="att_context",
    )(sink, qr, k2, v2)
    return jnp.concatenate([y_ctx, y_lat], axis=0)


def _merge_kernel(a1_ref, a2_ref, a3_ref, w1_ref, w2_ref, w3_ref, g1_ref, g2_ref, g3_ref, o_ref):
    acc = _sigmoid(g1_ref[...]) * _dot(a1_ref[...], w1_ref[...].astype(BF16))
    acc = acc + _sigmoid(g2_ref[...]) * _dot(a2_ref[...], w2_ref[...].astype(BF16))
    acc = acc + _sigmoid(g3_ref[...]) * _dot(a3_ref[...], w3_ref[...].astype(BF16))
    o_ref[...] = acc.astype(BF16)


def merge_branches(y_s5, y_gla, y_att, z, w_s5, w_gla, w_att, l):
    n = z.shape[0]
    tm = _pick(n, (528, 640, 256))
    tn = 512
    a_spec = pl.BlockSpec((tm, S5_WIDTH), lambda i, j: (i, 0))
    w_spec = pl.BlockSpec((None, S5_WIDTH, tn), lambda i, j: (l, 0, j))

    def g_spec(b):
        return pl.BlockSpec((tm, tn), lambda i, j: (i, (OFF_GATE + b * D_MODEL) // tn + j))

    return pl.pallas_call(
        _merge_kernel,
        grid=(n // tm, D_MODEL // tn),
        in_specs=[a_spec, a_spec, a_spec, w_spec, w_spec, w_spec, g_spec(0), g_spec(1), g_spec(2)],
        out_specs=pl.BlockSpec((tm, tn), lambda i, j: (i, j)),
        out_shape=jax.ShapeDtypeStruct((n, D_MODEL), BF16),
        compiler_params=_cp(("arbitrary", "arbitrary")),
        name="merge",
    )(y_s5, y_gla, y_att, w_s5, w_gla, w_att, z, z, z)


def _out_kernel(m_ref, w_ref, x_ref, mod_ref, g_ref, rt_ref, x1_ref, h2_ref, lg_ref, *, ctx_len, tm):
    i = pl.program_id(0)
    is_ctx = (i * tm + lax.broadcasted_iota(jnp.int32, (tm, 1), 0)) < ctx_len
    x1 = x_ref[...] + _row_mods(mod_ref, 2, is_ctx) * _dot(m_ref[...], w_ref[...])
    x1_ref[...] = x1
    r = lax.rsqrt(jnp.mean(x1 * x1, axis=-1, keepdims=True) + EPS)
    h2 = ((x1 * r) * g_ref[0]) * (1.0 + _row_mods(mod_ref, 4, is_ctx)) + _row_mods(mod_ref, 3, is_ctx)
    h_hi = h2.astype(BF16)
    h2_ref[...] = h_hi
    h_lo = (h2 - h_hi.astype(F32)).astype(BF16)
    rt = rt_ref[...]
    r_hi = rt.astype(BF16)
    r_lo = (rt - r_hi.astype(F32)).astype(BF16)
    lg_ref[...] = _dot(h_hi, r_hi) + (_dot(h_lo, r_hi) + _dot(h_hi, r_lo))


def out_projection(m, w_out_bf, xs, mods, norm_g, router_pad, l, ctx_len):
    n = xs.shape[0]
    tm = _pick(n, (528, 640, 256))
    once = pl.Buffered(1)
    return pl.pallas_call(
        functools.partial(_out_kernel, ctx_len=ctx_len, tm=tm),
        grid=(n // tm,),
        in_specs=[pl.BlockSpec((tm, D_MODEL), lambda i: (i, 0)),
                  pl.BlockSpec((None, D_MODEL, D_MODEL), lambda i: (l, 0, 0), pipeline_mode=once),
                  pl.BlockSpec((tm, D_MODEL), lambda i: (i, 0)),
                  pl.BlockSpec((None, 8, N_MOD * D_MODEL), lambda i: (l, 0, 0)),
                  pl.BlockSpec((1, 1, D_MODEL), lambda i: (l, 0, 0)),
                  pl.BlockSpec((None, D_MODEL, LANES), lambda i: (l, 0, 0), pipeline_mode=once)],
        out_specs=[pl.BlockSpec((tm, D_MODEL), lambda i: (i, 0)),
                   pl.BlockSpec((tm, D_MODEL), lambda i: (i, 0)),
                   pl.BlockSpec((tm, LANES), lambda i: (i, 0))],
        out_shape=[jax.ShapeDtypeStruct((n, D_MODEL), F32),
                   jax.ShapeDtypeStruct((n, D_MODEL), BF16),
                   jax.ShapeDtypeStruct((n, LANES), F32)],
        compiler_params=_cp(("arbitrary",)),
        name="out_projection",
    )(m, w_out_bf, xs, mods, norm_g.reshape(DEPTH, 1, D_MODEL), router_pad)


def _route_kernel(lg_ref, aff_ref, slot_ref, slot_t_ref, s0_ref, cnt_ref, *, ctx_len, n):
    tt = TOK_TILE
    lane = lax.broadcasted_iota(jnp.int32, (1, LANES), 1)
    is_expert = lane < N_EXPERTS

    def softmax_tile(b, _):
        r0 = pl.multiple_of(b * tt, tt)
        x = jnp.where(is_expert, lg_ref[pl.ds(r0, tt), :], -jnp.inf)
        e = jnp.exp(x - jnp.max(x, axis=-1, keepdims=True))
        aff_ref[pl.ds(r0, tt), :] = e / jnp.sum(e, axis=-1, keepdims=True)
        return 0

    lax.fori_loop(0, n // tt, softmax_tile, 0)

    ri = lax.broadcasted_iota(jnp.int32, (tt, tt), 0)
    ci = lax.broadcasted_iota(jnp.int32, (tt, tt), 1)
    before = jnp.where(ci < ri, 1.0, 0.0).astype(BF16)

    def bits_of(b):
        r0 = pl.multiple_of(b * tt, tt)
        return pltpu.bitcast(aff_ref[pl.ds(r0, tt), :], jnp.int32)

    def count(b0, b1, pred):
        def body(b, acc):
            return acc + jnp.sum(jnp.where(pred(bits_of(b)), 1.0, 0.0), axis=0, keepdims=True)
        return lax.fori_loop(b0, b1, body, jnp.zeros((1, LANES), F32))

    def route_set(b0, b1, slot_base):
        cap = float(EC_CAPACITY * (b1 - b0) * tt // N_EXPERTS)

        def bisect(_, lh):
            lo, hi = lh
            mid = lo + lax.shift_right_logical(hi - lo + 1, 1)
            ok = count(b0, b1, lambda v: v >= mid) >= cap
            return jnp.where(ok, mid, lo), jnp.where(ok, hi, mid - 1)

        lo0 = jnp.zeros((1, LANES), jnp.int32)
        hi0 = jnp.full((1, LANES), 0x7F800000, jnp.int32)
        thr, _ = lax.fori_loop(0, 31, bisect, (lo0, hi0))
        need = cap - count(b0, b1, lambda v: v > thr)

        def assign(b, carry):
            tie_seen, taken = carry
            r0 = pl.multiple_of(b * tt, tt)
            v = bits_of(b)
            tie = jnp.where(v == thr, 1.0, 0.0)
            tie_rank = _dot(before, tie.astype(BF16)) + tie_seen
            sel = jnp.where((v > thr) | ((v == thr) & (tie_rank < need)), 1.0, 0.0)
            rank = _dot(before, sel.astype(BF16)) + taken
            slot = jnp.where(sel > 0.0, rank + slot_base, -1.0)
            slot_ref[pl.ds(r0, tt), :] = slot.astype(jnp.int32)
            slot_t_ref[:, pl.ds(r0, tt)] = jnp.transpose(slot)[0:N_EXPERTS, :].astype(jnp.int32)
            n_sel = jnp.sum(sel, axis=0, keepdims=True)
            s0_ref[b] = (taken + slot_base).astype(jnp.int32)
            cnt_ref[b] = n_sel.astype(jnp.int32)
            return tie_seen + jnp.sum(tie, axis=0, keepdims=True), taken + n_sel

        zero = jnp.zeros((1, LANES), F32)
        lax.fori_loop(b0, b1, assign, (zero, zero))
        return cap

    cb = ctx_len // tt
    cap_ctx = route_set(0, cb, 0.0)
    route_set(cb, n // tt, cap_ctx)


def route(logits, ctx_len):
    n = logits.shape[0]
    nt = n // TOK_TILE
    full = lambda shape: pl.BlockSpec(shape, lambda i: tuple(0 for _ in shape))
    return pl.pallas_call(
        functools.partial(_route_kernel, ctx_len=ctx_len, n=n),
        grid=(1,),
        in_specs=[full((n, LANES))],
        out_specs=[full((n, LANES)), full((n, LANES)), full((N_EXPERTS, n)), full((nt, 1, LANES)),
                   full((nt, 1, LANES))],
        out_shape=[jax.ShapeDtypeStruct((n, LANES), F32),
                   jax.ShapeDtypeStruct((n, LANES), jnp.int32),
                   jax.ShapeDtypeStruct((N_EXPERTS, n), jnp.int32),
                   jax.ShapeDtypeStruct((nt, 1, LANES), jnp.int32),
                   jax.ShapeDtypeStruct((nt, 1, LANES), jnp.int32)],
        compiler_params=_cp(("arbitrary",)),
        name="route",
    )(logits)


def _window_plan(s0, cnt, rows, align_bits=3):
    a0 = lax.shift_left(lax.shift_right_logical(s0, align_bits), align_bits)
    nwin = jnp.where(cnt > 0, lax.shift_right_logical(s0 - a0 + cnt + MOE_WIN - 1, MOE_WIN.bit_length() - 1), 0)
    return a0, nwin, rows - MOE_WIN


GATHER_GROUP = 4
BF16_ROWS_LOG2 = 4


def _gather_kernel(s0_ref, cnt_ref, h_ref, slot_ref, xs_ref, *, rows):
    g = pl.program_id(0)
    t = pl.program_id(1)

    @pl.when(t == 0)
    def _():
        xs_ref[...] = jnp.zeros_like(xs_ref)

    for i in range(GATHER_GROUP):
        e = g * GATHER_GROUP + i
        a0, nwin, last = _window_plan(s0_ref[t, e], cnt_ref[t, e], rows, BF16_ROWS_LOG2)
        slot = slot_ref[i]

        def body(j, _, i=i, a0=a0, last=last, slot=slot):
            start = a0 + j * MOE_WIN
            base = pl.multiple_of(jnp.minimum(start, last), 1 << BF16_ROWS_LOG2)
            want = base + lax.broadcasted_iota(jnp.int32, (MOE_WIN, 1), 0)
            want = jnp.where(want >= start, want, -2)
            onehot = jnp.where(slot == want, 1.0, 0.0).astype(BF16)
            prev = xs_ref[i, pl.ds(base, MOE_WIN), :].astype(F32)
            xs_ref[i, pl.ds(base, MOE_WIN), :] = (prev + _dot(onehot, h_ref[...])).astype(BF16)
            return 0

        lax.fori_loop(0, nwin, body, 0)


def moe_gather(h2, slot_t, s0, cnt, rows):
    n = h2.shape[0]
    nt = n // TOK_TILE
    gg = GATHER_GROUP
    return pl.pallas_call(
        functools.partial(_gather_kernel, rows=rows),
        grid_spec=pltpu.PrefetchScalarGridSpec(
            num_scalar_prefetch=2, grid=(N_EXPERTS // gg, nt),
            in_specs=[pl.BlockSpec((TOK_TILE, D_MODEL), lambda g, t, a, b: (t, 0)),
                      pl.BlockSpec((gg, 1, TOK_TILE), lambda g, t, a, b: (g, 0, t))],
            out_specs=pl.BlockSpec((gg, rows, D_MODEL), lambda g, t, a, b: (g, 0, 0))),
        out_shape=jax.ShapeDtypeStruct((N_EXPERTS, rows, D_MODEL), BF16),
        compiler_params=_cp(("arbitrary", "arbitrary")),
        name="moe_gather",
    )(s0, cnt, h2, slot_t.reshape(N_EXPERTS, 1, n))


def _ffn_kernel(xs_ref, wg_ref, wu_ref, wd_ref, y_ref, act_scr):
    f = pl.program_id(1)
    nf = EXPERT_FF // FF_TILE

    @pl.when(f < nf)
    def _():
        x = xs_ref[0]
        a = _dot(x, wg_ref[0].astype(BF16))
        u = _dot(x, wu_ref[0].astype(BF16))
        act_scr[:, pl.ds(pl.multiple_of(f * FF_TILE, FF_TILE), FF_TILE)] = ((a * _sigmoid(a)) * u).astype(BF16)

    @pl.when(f >= nf)
    def _():
        y_ref[0] = _dot(act_scr[...], wd_ref[0].astype(BF16)).astype(BF16)


def moe_ffn(xs, w_gate, w_up, w_down, l):
    rows = xs.shape[1]
    nf = EXPERT_FF // FF_TILE
    up_idx = lambda e, f: (l, e, 0, jnp.minimum(f, nf - 1))
    return pl.pallas_call(
        _ffn_kernel,
        grid=(N_EXPERTS, nf + D_MODEL // FF_TILE),
        in_specs=[pl.BlockSpec((1, rows, D_MODEL), lambda e, f: (e, 0, 0)),
                  pl.BlockSpec((None, 1, D_MODEL, FF_TILE), up_idx),
                  pl.BlockSpec((None, 1, D_MODEL, FF_TILE), up_idx),
                  pl.BlockSpec((None, 1, EXPERT_FF, FF_TILE), lambda e, f: (l, e, 0, jnp.maximum(f - nf, 0)))],
        out_specs=pl.BlockSpec((1, rows, FF_TILE), lambda e, f: (e, 0, jnp.maximum(f - nf, 0))),
        out_shape=jax.ShapeDtypeStruct((N_EXPERTS, rows, D_MODEL), BF16),
        scratch_shapes=[pltpu.VMEM((rows, EXPERT_FF), BF16)],
        compiler_params=_cp(("arbitrary", "arbitrary")),
        name="moe_ffn",
    )(xs, w_gate, w_up, w_down)


MORE_WINS = TOK_TILE // MOE_WIN


def _combine_kernel(s0_ref, cnt_ref, x_ref, slot_ref, aff_ref, mod_ref, g_ref, y_hbm, o_ref, buf0, bufx, sem0, semx,
                    *, rows, ctx_len, final):
    t = pl.program_id(0)
    nt = pl.num_programs(0)
    tt = TOK_TILE

    def first_copy(tile, e, par):
        a0, _, last = _window_plan(s0_ref[tile, e], cnt_ref[tile, e], rows)
        base = pl.multiple_of(jnp.minimum(a0, last), 8)
        cp = pltpu.make_async_copy(y_hbm.at[e, pl.ds(base, MOE_WIN)],
                                   buf0.at[par, pl.ds(e * MOE_WIN, MOE_WIN)], sem0.at[par, e])
        return base, a0, cp

    def more_copy(e, j, a0, last):
        base = pl.multiple_of(jnp.minimum(a0 + j * MOE_WIN, last), 8)
        k = e * MORE_WINS + j - 1
        return base, pltpu.make_async_copy(y_hbm.at[e, pl.ds(base, MOE_WIN)], bufx.at[k], semx.at[k])

    @pl.when(t == 0)
    def _():
        for e in range(N_EXPERTS):
            first_copy(0, e, 0)[2].start()

    @pl.when(t + 1 < nt)
    def _():
        for e in range(N_EXPERTS):
            first_copy(t + 1, e, (t + 1) % 2)[2].start()

    for e in range(N_EXPERTS):
        a0, nwin, last = _window_plan(s0_ref[t, e], cnt_ref[t, e], rows)

        def start(j, _, e=e, a0=a0, last=last):
            more_copy(e, j, a0, last)[1].start()
            return 0

        lax.fori_loop(1, nwin, start, 0)

    par = t % 2
    lane = lax.broadcasted_iota(jnp.int32, (1, LANES), 1)
    left = lane < MOE_WIN
    offs = jnp.where(left, lane, lane - MOE_WIN)
    cols_hi, cols_lo = [], []
    for c in range(N_EXPERTS * MOE_WIN // LANES):
        e0, e1 = 2 * c, 2 * c + 1
        b0, a00, cp0 = first_copy(t, e0, par)
        b1, a01, cp1 = first_copy(t, e1, par)
        cp0.wait()
        cp1.wait()
        want = jnp.where(left, b0, b1) + offs
        want = jnp.where(want >= jnp.where(left, a00, a01), want, -2)
        picked = jnp.where(left, slot_ref[:, e0:e0 + 1], slot_ref[:, e1:e1 + 1]) == want
        gate = jnp.where(picked, jnp.where(left, aff_ref[:, e0:e0 + 1], aff_ref[:, e1:e1 + 1]), 0.0)
        g_hi = gate.astype(BF16)
        cols_hi.append(g_hi)
        cols_lo.append((gate - g_hi.astype(F32)).astype(BF16))
    y0 = buf0[par]
    o_ref[...] = _dot(jnp.concatenate(cols_hi, axis=-1), y0) + _dot(jnp.concatenate(cols_lo, axis=-1), y0)

    for e in range(N_EXPERTS):
        a0, nwin, last = _window_plan(s0_ref[t, e], cnt_ref[t, e], rows)

        def take(j, _, e=e, a0=a0, last=last):
            base, cp = more_copy(e, j, a0, last)
            cp.wait()
            want = base + lax.broadcasted_iota(jnp.int32, (1, MOE_WIN), 1)
            want = jnp.where(want >= a0 + j * MOE_WIN, want, -2)
            onehot = jnp.where(slot_ref[:, e:e + 1] == want, 1.0, 0.0).astype(BF16)
            o_ref[...] += aff_ref[:, e:e + 1] * _dot(onehot, bufx[e * MORE_WINS + j - 1])
            return 0

        lax.fori_loop(1, nwin, take, 0)

    is_ctx = (t * tt + lax.broadcasted_iota(jnp.int32, (tt, 1), 0)) < ctx_len
    x2 = x_ref[...] + _row_mods(mod_ref, 5, is_ctx) * o_ref[...]
    if final:
        r = lax.rsqrt(jnp.mean(x2 * x2, axis=-1, keepdims=True) + EPS)
        x2 = (x2 * r) * g_ref[...]
    o_ref[...] = x2


def moe_combine(x1, slot, aff, s0, cnt, mods, y, final_g, l, ctx_len, final):
    n = x1.shape[0]
    rows = y.shape[1]
    tt = TOK_TILE
    nbuf = N_EXPERTS * MORE_WINS
    return pl.pallas_call(
        functools.partial(_combine_kernel, rows=rows, ctx_len=ctx_len, final=final),
        grid_spec=pltpu.PrefetchScalarGridSpec(
            num_scalar_prefetch=2, grid=(n // tt,),
            in_specs=[pl.BlockSpec((tt, D_MODEL), lambda t, a, b: (t, 0)),
                      pl.BlockSpec((tt, LANES), lambda t, a, b: (t, 0)),
                      pl.BlockSpec((tt, LANES), lambda t, a, b: (t, 0)),
                      pl.BlockSpec((None, 8, N_MOD * D_MODEL), lambda t, a, b: (l, 0, 0)),
                      pl.BlockSpec((1, D_MODEL), lambda t, a, b: (0, 0)),
                      pl.BlockSpec(memory_space=pl.ANY)],
            out_specs=pl.BlockSpec((tt, D_MODEL), lambda t, a, b: (t, 0)),
            scratch_shapes=[pltpu.VMEM((2, N_EXPERTS * MOE_WIN, D_MODEL), BF16),
                            pltpu.VMEM((nbuf, MOE_WIN, D_MODEL), BF16),
                            pltpu.SemaphoreType.DMA((2, N_EXPERTS)),
                            pltpu.SemaphoreType.DMA((nbuf,))]),
        out_shape=jax.ShapeDtypeStruct((n, D_MODEL), F32),
        compiler_params=_cp(("arbitrary",)),
        name="moe_combine",
    )(s0, cnt, x1, slot, aff, mods, final_g.reshape(1, D_MODEL), y)


def rope_tables(seq, ctx_len):
    pos = jnp.arange(seq)
    row = (pos // GRID_W).astype(F32)
    col = (pos % GRID_W).astype(F32)
    n_freq = ATT_HEAD_DIM // 4
    inv_freq = ROPE_BASE ** (-jnp.arange(n_freq, dtype=F32) / n_freq)
    ang = jnp.concatenate([row[:, None] * inv_freq, col[:, None] * inv_freq], axis=-1)
    cos, sin = jnp.cos(ang), jnp.sin(ang)
    cos_t = jnp.concatenate([jnp.ones((ctx_len, LANES), F32), jnp.tile(cos, (1, 4))], axis=0)
    sin_t = jnp.concatenate([jnp.zeros((ctx_len, LANES), F32), jnp.tile(jnp.concatenate([-sin, sin], -1), (1, 2))],
                            axis=0)
    return cos_t, sin_t


def kernel(x, c, ctx, c_ctx, ada_w, ada_b, norm1_g, norm2_g, w_in, s5_lam_re, s5_lam_im, s5_log_dt,
           s5_b_re, s5_b_im, s5_c_re, s5_c_im, s5_d, s5_w_glu, gla_w1, gla_w2, gla_b, gla_norm_g,
           attn_sink, w_branch_s5, w_branch_gla, w_branch_attn, w_out, moe_router, moe_w_gate,
           moe_w_up, moe_w_down, final_g):
    seq, ctx_len = x.shape[1], ctx.shape[1]
    n = seq + ctx_len
    assert x.shape[0] == 1 and ctx_len % TOK_TILE == 0 and seq % TOK_TILE == 0
    xs = jnp.concatenate([ctx[0], x[0]], axis=0)
    c8 = jnp.zeros((8, D_MODEL), F32).at[0].set(c[0]).at[1].set(c_ctx)
    mods = ada_mods(c8, ada_w, ada_b)
    cos_t, sin_t = rope_tables(seq, ctx_len)
    w_out_bf = w_out.astype(BF16)
    router_pad = jnp.pad(moe_router, ((0, 0), (0, 0), (0, LANES - N_EXPERTS)))
    rows = EC_CAPACITY * n // N_EXPERTS
    for l in range(DEPTH):
        w1cat = jnp.pad(jnp.concatenate([gla_w1[l, 0], gla_w1[l, 1]], axis=-1),
                        ((0, 0), (0, LANES - 2 * GLA_RANK)))
        w2pad = jnp.zeros((2, LANES, GLA_KW), F32)
        w2pad = w2pad.at[0, 0:GLA_RANK].set(gla_w2[l, 0]).at[1, GLA_RANK:2 * GLA_RANK].set(gla_w2[l, 1])
        z, lr = in_projection(xs, norm1_g, mods, w_in, w1cat, l, ctx_len)
        s5w = s5_weights(s5_lam_re[l], s5_lam_im[l], s5_log_dt[l], s5_b_re[l], s5_b_im[l], s5_c_re[l], s5_c_im[l])
        y_s5 = s5_glu(s5_mix(z, s5w, s5_d[l], ctx_len), s5_w_glu, l)
        gla_bias = gla_b[l].reshape(2, 1, GLA_KW)
        o_f = gla_direction(z, lr, w2pad, gla_bias, 0, ctx_len)
        o_b = gla_direction(z, lr, w2pad, gla_bias, 1, ctx_len)
        y_gla = gla_post(o_f, o_b, z, gla_norm_g, l)
        qr, k2, v2 = att_prep(z, cos_t, sin_t)
        y_att = attention(qr, k2, v2, attn_sink[l], ctx_len)
        m = merge_branches(y_s5, y_gla, y_att, z, w_branch_s5, w_branch_gla, w_branch_attn, l)
        x1, h2, logits = out_projection(m, w_out_bf, xs, mods, norm2_g, router_pad, l, ctx_len)
        aff, slot, slot_t, s0, cnt = route(logits, ctx_len)
        s0, cnt = s0[:, 0, :N_EXPERTS], cnt[:, 0, :N_EXPERTS]
        xe = moe_gather(h2, slot_t, s0, cnt, rows)
        ye = moe_ffn(xe, moe_w_gate, moe_w_up, moe_w_down, l)
        xs = moe_combine(x1, slot, aff, s0, cnt, mods, ye, final_g, l, ctx_len, l == DEPTH - 1)
    return xs[ctx_len:][None]
```

```python
import functools
import math

import jax
import jax.numpy as jnp
from jax import lax
from jax.experimental import pallas as pl
from jax.experimental.pallas import tpu as pltpu

D_MODEL = 2048
DEPTH = 2
GRID_W = 64
EPS = 1e-6
N_MOD = 6
S5_WIDTH = 1024
S5_GROUP_CH = 16
S5_GROUPS = S5_WIDTH // S5_GROUP_CH
S5_STATE = 64
GLA_HEADS = 4
GLA_DK = 128
GLA_DV = 256
GLA_KW = GLA_HEADS * GLA_DK
GLA_VW = GLA_HEADS * GLA_DV
GLA_RANK = 16
GLA_TAU = 16.0
GLA_CHUNK = 64
ATT_HEADS = 16
ATT_KV_HEADS = 4
ATT_HEAD_DIM = 64
ATT_QW = ATT_HEADS * ATT_HEAD_DIM
ATT_KVW = ATT_KV_HEADS * ATT_HEAD_DIM
WINDOW = 128
ATT_BLOCK = 128
ROPE_BASE = 10000.0
N_BRANCH = 3
IN_SPLITS = (S5_WIDTH, GLA_KW, GLA_KW, GLA_VW, GLA_VW, ATT_QW, ATT_KVW, ATT_KVW, N_BRANCH * D_MODEL)
IN_WIDTH = sum(IN_SPLITS)
N_EXPERTS = 16
EXPERT_FF = 2048
EC_CAPACITY = 2

OFF_U = 0
OFF_GQ = OFF_U + S5_WIDTH
OFF_GK = OFF_GQ + GLA_KW
OFF_GV = OFF_GK + GLA_KW
OFF_GR = OFF_GV + GLA_VW
OFF_AQ = OFF_GR + GLA_VW
OFF_AK = OFF_AQ + ATT_QW
OFF_AV = OFF_AK + ATT_KVW
OFF_GATE = OFF_AV + ATT_KVW

LANES = 128
S5_T = 16
S5_PAIRS = S5_GROUPS // 2
S5_BLOCKS = S5_WIDTH // LANES
S5_BLK_GROUPS = LANES // S5_GROUP_CH
S5_BLK_PAIRS = S5_BLK_GROUPS // 2
S5_BLK_STATE = S5_BLK_GROUPS * S5_STATE
TOK_TILE = 256
MOE_WIN = 64
FF_TILE = 512
VMEM_MB = 56

F32 = jnp.float32
BF16 = jnp.bfloat16
HI = lax.Precision.HIGHEST


def _cp(sem, mb=VMEM_MB):
    return pltpu.CompilerParams(dimension_semantics=sem, vmem_limit_bytes=mb << 20)


def _dot(a, b):
    return jnp.dot(a, b, preferred_element_type=F32)


def _dot_nt(a, b):
    return lax.dot_general(a, b, (((1,), (1,)), ((), ())), preferred_element_type=F32)


def _dot_tn(a, b):
    return lax.dot_general(a, b, (((0,), (0,)), ((), ())), preferred_element_type=F32)


def _pick(n, cands):
    for c in cands:
        if n % c == 0:
            return c
    raise ValueError(f"no tile for {n} in {cands}")


def _sigmoid(x):
    return 1.0 / (1.0 + jnp.exp(-x))


def _row_mods(mod_ref, k, is_ctx):
    lo, hi = k * D_MODEL, (k + 1) * D_MODEL
    return jnp.where(is_ctx, mod_ref[1:2, lo:hi], mod_ref[0:1, lo:hi])


def _ada_kernel(c_ref, w_ref, b_ref, o_ref):
    cc = c_ref[...]
    o_ref[0] = jnp.dot(cc * _sigmoid(cc), w_ref[0], preferred_element_type=F32, precision=HI) + b_ref[0]


def ada_mods(c8, ada_w, ada_b):
    tn = 1024
    width = N_MOD * D_MODEL
    return pl.pallas_call(
        _ada_kernel,
        grid=(DEPTH, width // tn),
        in_specs=[pl.BlockSpec((8, D_MODEL), lambda l, j: (0, 0)),
                  pl.BlockSpec((1, D_MODEL, tn), lambda l, j: (l, 0, j)),
                  pl.BlockSpec((1, 1, tn), lambda l, j: (l, 0, j))],
        out_specs=pl.BlockSpec((1, 8, tn), lambda l, j: (l, 0, j)),
        out_shape=jax.ShapeDtypeStruct((DEPTH, 8, width), F32),
        compiler_params=_cp(("arbitrary", "arbitrary")),
        name="ada_mods",
    )(c8, ada_w, ada_b.reshape(DEPTH, 1, width))


def _in_kernel(x_ref, g_ref, mod_ref, w_ref, w1_ref, z_ref, lr_ref, h_scr, *, ctx_len, tm):
    i = pl.program_id(0)
    j = pl.program_id(1)

    @pl.when(j == 0)
    def _():
        x = x_ref[...]
        r = lax.rsqrt(jnp.mean(x * x, axis=-1, keepdims=True) + EPS)
        hn = (x * r) * g_ref[0]
        is_ctx = (i * tm + lax.broadcasted_iota(jnp.int32, (tm, 1), 0)) < ctx_len
        hb = (hn * (1.0 + _row_mods(mod_ref, 1, is_ctx)) + _row_mods(mod_ref, 0, is_ctx)).astype(BF16)
        h_scr[...] = hb
        lr_ref[...] = _dot(hb, w1_ref[...].astype(BF16))

    z_ref[...] = _dot(h_scr[...], w_ref[...].astype(BF16))


def in_projection(xs, norm_g, mods, w_in, w1cat, l, ctx_len):
    n = xs.shape[0]
    tm = _pick(n, (1056, 640, 256))
    tn = 512
    return pl.pallas_call(
        functools.partial(_in_kernel, ctx_len=ctx_len, tm=tm),
        grid=(n // tm, IN_WIDTH // tn),
        in_specs=[pl.BlockSpec((tm, D_MODEL), lambda i, j: (i, 0)),
                  pl.BlockSpec((1, 1, D_MODEL), lambda i, j: (l, 0, 0)),
                  pl.BlockSpec((None, 8, N_MOD * D_MODEL), lambda i, j: (l, 0, 0)),
                  pl.BlockSpec((None, D_MODEL, tn), lambda i, j: (l, 0, j)),
                  pl.BlockSpec((D_MODEL, LANES), lambda i, j: (0, 0))],
        out_specs=[pl.BlockSpec((tm, tn), lambda i, j: (i, j)),
                   pl.BlockSpec((tm, LANES), lambda i, j: (i, 0))],
        out_shape=[jax.ShapeDtypeStruct((n, IN_WIDTH), F32),
                   jax.ShapeDtypeStruct((n, LANES), F32)],
        scratch_shapes=[pltpu.VMEM((tm, D_MODEL), BF16)],
        compiler_params=_cp(("arbitrary", "arbitrary")),
        name="in_projection",
    )(xs, norm_g.reshape(DEPTH, 1, D_MODEL), mods, w_in, w1cat)


def s5_params(lam_re, lam_im, log_dt, b_re, b_im):
    dt = jnp.exp(log_dt)[..., None]
    mag = jnp.exp(lam_re * dt)
    ab_re = mag * jnp.cos(lam_im * dt)
    ab_im = mag * jnp.sin(lam_im * dt)
    den = lam_re * lam_re + lam_im * lam_im
    nr = ab_re - 1
    f_re = ((nr * lam_re + ab_im * lam_im) / den)[:, :, None, :]
    f_im = ((ab_im * lam_re - nr * lam_im) / den)[:, :, None, :]
    bt_re, bt_im = b_re.transpose(0, 1, 3, 2), b_im.transpose(0, 1, 3, 2)
    bb_re = f_re * bt_re - f_im * bt_im
    bb_im = f_re * bt_im + f_im * bt_re
    tau = jnp.arange(S5_T + 1, dtype=F32)[None, :, None, None]
    pmag = jnp.exp((lam_re * dt)[:, None] * tau)
    ang = (lam_im * dt)[:, None] * tau
    return pmag * jnp.cos(ang), pmag * jnp.sin(ang), bb_re, bb_im


def _s5_chunk_rows(z_ref, nch):
    return jnp.concatenate([z_ref[pl.ds(t, nch, stride=S5_T), :].astype(BF16) for t in range(S5_T)], axis=-1)


def _s5_place(stage, pr, pi, wre_ref, wim_ref, d, im_sign):
    ch, n_st = S5_GROUP_CH, S5_STATE
    for g in range(S5_BLK_GROUPS):
        a_r, a_i = pr[g:g + 1, :], pi[g:g + 1, :]
        w_r, w_i = wre_ref[d, g], wim_ref[d, g]
        stage[g * ch:(g + 1) * ch, g * n_st:(g + 1) * n_st] = a_r * w_r - a_i * w_i
        stage[g * ch:(g + 1) * ch, S5_BLK_STATE + g * n_st:S5_BLK_STATE + (g + 1) * n_st] = (
            im_sign * (a_r * w_i + a_i * w_r))


def _s5_state_in_kernel(z_ref, pre_ref, pim_ref, bre_ref, bim_ref, vre_ref, vim_ref, wq_scr, stage, *, nch):
    @pl.when((pl.program_id(0) == 0) & (pl.program_id(1) == 0))
    def _():
        stage[...] = jnp.zeros_like(stage)

    rev = pl.program_id(1) == 1
    for t in range(S5_T):
        e_t = jnp.where(rev, t, S5_T - 1 - t)
        _s5_place(stage, pre_ref[0, e_t], pim_ref[0, e_t], bre_ref, bim_ref, 0, 1.0)
        wq_scr[t * LANES:(t + 1) * LANES, :] = stage[...].astype(BF16)
    v = _dot(_s5_chunk_rows(z_ref, nch), wq_scr[...])
    vre_ref[0] = v[:, :S5_BLK_STATE]
    vim_ref[0] = v[:, S5_BLK_STATE:]


def _s5_scan_kernel(vre_ref, vim_ref, are_ref, aim_ref, sre_ref, sim_ref, *, nch, ncc):
    rev = pl.program_id(0) == 1
    ar = are_ref[0]
    ai = aim_ref[0]

    def body(i, carry):
        sr, si = carry
        k = jnp.where(rev, jnp.where(i < ncc, ncc - 1 - i, nch + ncc - 1 - i), i)
        sre_ref[0, k] = sr
        sim_ref[0, k] = si
        return ar * sr - ai * si + vre_ref[0, k], ar * si + ai * sr + vim_ref[0, k]

    zero = jnp.zeros((S5_BLOCKS, LANES), F32)
    lax.fori_loop(0, nch, body, (zero, zero))


def _s5_out_kernel(z_ref, sre_ref, sim_ref, pre_ref, pim_ref, bre_ref, bim_ref, cre_ref, cim_ref, d_ref, y_ref,
                   wq_scr, wt_scr, stage, bb_scr, *, nch):
    @pl.when(pl.program_id(0) == 0)
    def _():
        stage[...] = jnp.zeros_like(stage)
        bb_scr[...] = jnp.zeros_like(bb_scr)

    one = jnp.ones((1, S5_STATE), F32)
    zero = jnp.zeros((1, S5_STATE), F32)
    ones_re = jnp.concatenate([one] * S5_BLK_GROUPS, axis=0)
    zeros_im = jnp.concatenate([zero] * S5_BLK_GROUPS, axis=0)
    for d in range(2):
        _s5_place(bb_scr.at[d], ones_re, zeros_im, bre_ref, bim_ref, d, 1.0)
        for tau in range(S5_T + 1):
            _s5_place(stage, pre_ref[d, tau], pim_ref[d, tau], cre_ref, cim_ref, d, -1.0)
            blk = stage[...]
            if tau < S5_T:
                k = S5_T - 1 - tau if d == 0 else tau
                kq = lax.dot_general(bb_scr[d], blk, (((1,), (1,)), ((), ())), preferred_element_type=F32,
                                     precision=HI)
                wt_scr[d, k * LANES:(k + 1) * LANES, :] = kq.astype(BF16)
            if tau > 0:
                t = tau - 1 if d == 0 else S5_T - tau
                wq_scr[d, t * LANES:(t + 1) * LANES, :] = blk.astype(BF16)
    rows = _s5_chunk_rows(z_ref, nch)
    sp = [jnp.concatenate([sre_ref[d], sim_ref[d]], axis=-1).astype(BF16) for d in range(2)]
    for t in range(S5_T):
        acc = z_ref[pl.ds(t, nch, stride=S5_T), :] * d_ref[...]
        acc = acc + _dot(rows[:, :LANES * (t + 1)], wt_scr[0, LANES * (S5_T - 1 - t):, :])
        acc = acc + _dot(rows[:, LANES * t:], wt_scr[1, :LANES * (S5_T - t), :])
        for d in range(2):
            acc = acc + _dot_nt(sp[d], wq_scr[d, t * LANES:(t + 1) * LANES, :])
        y_ref[pl.ds(t, nch, stride=S5_T), :] = acc


def s5_mix(z, params, c_re, c_im, d_skip, ctx_len):
    n = z.shape[0]
    nch, ncc = n // S5_T, ctx_len // S5_T
    p_re, p_im, bb_re, bb_im = params
    st, tl = S5_BLK_STATE, S5_T * LANES
    gb, ch, n_st = S5_BLK_GROUPS, S5_GROUP_CH, S5_STATE
    a_re = p_re[:, S5_T].reshape(2, S5_BLOCKS, st)
    a_im = p_im[:, S5_T].reshape(2, S5_BLOCKS, st)
    v_shape = jax.ShapeDtypeStruct((2, nch, S5_BLOCKS * st), F32)
    pow_spec = pl.BlockSpec((1, S5_T + 1, gb, n_st), lambda q, d: (d, 0, q, 0))
    par_spec = pl.BlockSpec((1, gb, ch, n_st), lambda q, d: (d, q, 0, 0))
    v_re, v_im = pl.pallas_call(
        functools.partial(_s5_state_in_kernel, nch=nch),
        grid=(S5_BLOCKS, 2),
        in_specs=[pl.BlockSpec((n, LANES), lambda q, d: (0, q)), pow_spec, pow_spec, par_spec, par_spec],
        out_specs=[pl.BlockSpec((1, nch, st), lambda q, d: (d, 0, q))] * 2,
        out_shape=[v_shape, v_shape],
        scratch_shapes=[pltpu.VMEM((tl, 2 * st), BF16), pltpu.VMEM((LANES, 2 * st), F32)],
        compiler_params=_cp(("arbitrary", "arbitrary")),
        name="s5_state_in",
    )(z, p_re, p_im, bb_re, bb_im)
    blocked = (2, nch, S5_BLOCKS, st)
    v_spec = pl.BlockSpec((1, nch, S5_BLOCKS, LANES), lambda d, j: (d, 0, 0, j))
    a_spec = pl.BlockSpec((1, S5_BLOCKS, LANES), lambda d, j: (d, 0, j))
    s_shape = jax.ShapeDtypeStruct(blocked, F32)
    s_re, s_im = pl.pallas_call(
        functools.partial(_s5_scan_kernel, nch=nch, ncc=ncc),
        grid=(2, st // LANES),
        in_specs=[v_spec, v_spec, a_spec, a_spec],
        out_specs=[v_spec, v_spec],
        out_shape=[s_shape, s_shape],
        compiler_params=_cp(("arbitrary", "arbitrary")),
        name="s5_scan",
    )(v_re.reshape(blocked), v_im.reshape(blocked), a_re, a_im)
    s_spec = pl.BlockSpec((2, nch, st), lambda q: (0, 0, q))
    pow2_spec = pl.BlockSpec((2, S5_T + 1, gb, n_st), lambda q: (0, 0, q, 0))
    par2_spec = pl.BlockSpec((2, gb, ch, n_st), lambda q: (0, q, 0, 0))
    return pl.pallas_call(
        functools.partial(_s5_out_kernel, nch=nch),
        grid=(S5_BLOCKS,),
        in_specs=[pl.BlockSpec((n, LANES), lambda q: (0, q)), s_spec, s_spec, pow2_spec, pow2_spec,
                  par2_spec, par2_spec, par2_spec, par2_spec, pl.BlockSpec((1, LANES), lambda q: (0, q))],
        out_specs=pl.BlockSpec((n, LANES), lambda q: (0, q)),
        out_shape=jax.ShapeDtypeStruct((n, S5_WIDTH), F32),
        scratch_shapes=[pltpu.VMEM((2, tl, 2 * st), BF16), pltpu.VMEM((2, tl, LANES), BF16),
                        pltpu.VMEM((LANES, 2 * st), F32), pltpu.VMEM((2, LANES, 2 * st), F32)],
        compiler_params=_cp(("arbitrary",)),
        name="s5_out",
    )(z, s_re.reshape(2, nch, S5_BLOCKS * st), s_im.reshape(2, nch, S5_BLOCKS * st), p_re, p_im,
      bb_re, bb_im, c_re, c_im, d_skip.reshape(1, S5_WIDTH))


def _s5_glu_kernel(y_ref, w_ref, o_ref):
    y = y_ref[...]
    g = 0.5 * y * (1.0 + jnp.tanh(math.sqrt(2.0 / math.pi) * (y + 0.044715 * (y * y * y))))
    o_ref[...] = (g * _sigmoid(_dot(g.astype(BF16), w_ref[...].astype(BF16)))).astype(BF16)


def s5_glu(y, w_glu, l):
    n = y.shape[0]
    tm = _pick(n, (1056, 640, 256))
    return pl.pallas_call(
        _s5_glu_kernel,
        grid=(n // tm,),
        in_specs=[pl.BlockSpec((tm, S5_WIDTH), lambda i: (i, 0)),
                  pl.BlockSpec((None, S5_WIDTH, S5_WIDTH), lambda i: (l, 0, 0))],
        out_specs=pl.BlockSpec((tm, S5_WIDTH), lambda i: (i, 0)),
        out_shape=jax.ShapeDtypeStruct((n, S5_WIDTH), BF16),
        compiler_params=_cp(("arbitrary",)),
        name="s5_glu",
    )(y, w_glu)


def _gla_kernel(q_ref, k_ref, v_ref, lr_ref, w2_ref, b_ref, o_ref, st_scr, *, rev):
    c = GLA_CHUNK

    @pl.when(pl.program_id(0) == 0)
    def _():
        st_scr[...] = jnp.zeros_like(st_scr)

    x = _dot(lr_ref[...].astype(BF16), w2_ref[0].astype(BF16)) + b_ref[0]
    la = (jnp.minimum(x, 0.0) - jnp.log1p(jnp.exp(-jnp.abs(x)))) * (1.0 / GLA_TAU)
    ri = lax.broadcasted_iota(jnp.int32, (c, c), 0)
    ci = lax.broadcasted_iota(jnp.int32, (c, c), 1)
    keep = (ci >= ri) if rev else (ci <= ri)
    tri = jnp.where(keep, 1.0, 0.0).astype(BF16)
    hi = la.astype(BF16)
    r1 = la - hi.astype(F32)
    mid = r1.astype(BF16)
    lo = (r1 - mid.astype(F32)).astype(BF16)
    bc = _dot(tri, hi) + _dot(tri, mid) + _dot(tri, lo)
    last, ref_row = (0, c // 2) if rev else (c - 1, c // 2 - 1)
    btot = bc[last:last + 1]
    bmid = bc[ref_row:ref_row + 1]
    qs = q_ref[...] * (GLA_DK ** -0.5)
    kk = k_ref[...]
    q_mid = (qs * jnp.exp(bc - bmid)).astype(BF16)
    k_mid = (kk * jnp.exp(bmid - bc)).astype(BF16)
    q_in = (qs * jnp.exp(bc)).astype(BF16)
    k_out = (kk * jnp.exp(btot - bc)).astype(BF16)
    decay = jnp.exp(btot)
    for h in range(GLA_HEADS):
        ks = slice(h * GLA_DK, (h + 1) * GLA_DK)
        vs = slice(h * GLA_DV, (h + 1) * GLA_DV)
        vb = v_ref[:, vs].astype(BF16)
        sc = jnp.where(keep, _dot_nt(q_mid[:, ks], k_mid[:, ks]), 0.0).astype(BF16)
        st = st_scr[h]
        o_ref[:, vs] = _dot(sc, vb) + _dot_nt(q_in[:, ks], st.astype(BF16))
        st_scr[h] = decay[:, ks] * st + _dot_tn(vb, k_out[:, ks])


def gla_direction(z, lr, w2pad, bias, d, ctx_len):
    n = z.shape[0]
    c = GLA_CHUNK
    nch, ncc = n // c, ctx_len // c
    if d == 0:
        row = lambda i: i
    else:
        row = lambda i: jnp.where(i < ncc, ncc - 1 - i, nch + ncc - 1 - i)
    return pl.pallas_call(
        functools.partial(_gla_kernel, rev=d == 1),
        grid=(nch,),
        in_specs=[pl.BlockSpec((c, GLA_KW), lambda i: (row(i), OFF_GQ // GLA_KW)),
                  pl.BlockSpec((c, GLA_KW), lambda i: (row(i), OFF_GK // GLA_KW)),
                  pl.BlockSpec((c, GLA_VW), lambda i: (row(i), OFF_GV // GLA_VW)),
                  pl.BlockSpec((c, LANES), lambda i: (row(i), 0)),
                  pl.BlockSpec((1, LANES, GLA_KW), lambda i: (d, 0, 0)),
                  pl.BlockSpec((1, 1, GLA_KW), lambda i: (d, 0, 0))],
        out_specs=pl.BlockSpec((c, GLA_VW), lambda i: (row(i), 0)),
        out_shape=jax.ShapeDtypeStruct((n, GLA_VW), F32),
        scratch_shapes=[pltpu.VMEM((GLA_HEADS, GLA_DV, GLA_DK), F32)],
        compiler_params=_cp(("arbitrary",)),
        name=f"gla_dir{d}",
    )(z, z, z, lr, w2pad, bias)


def _gla_post_kernel(of_ref, ob_ref, r_ref, g_ref, y_ref):
    for h in range(GLA_HEADS):
        vs = slice(h * GLA_DV, (h + 1) * GLA_DV)
        o = of_ref[:, vs] + ob_ref[:, vs]
        rn = lax.rsqrt(jnp.mean(o * o, axis=-1, keepdims=True) + EPS)
        r = r_ref[:, vs]
        y_ref[:, vs] = ((o * rn) * g_ref[0, :, vs] * (r * _sigmoid(r))).astype(BF16)


def gla_post(o_f, o_b, z, norm_g, l):
    n = z.shape[0]
    tm = _pick(n, (1056, 640, 256))
    return pl.pallas_call(
        _gla_post_kernel,
        grid=(n // tm,),
        in_specs=[pl.BlockSpec((tm, GLA_VW), lambda i: (i, 0)),
                  pl.BlockSpec((tm, GLA_VW), lambda i: (i, 0)),
                  pl.BlockSpec((tm, GLA_VW), lambda i: (i, OFF_GR // GLA_VW)),
                  pl.BlockSpec((1, 1, GLA_VW), lambda i: (l, 0, 0))],
        out_specs=pl.BlockSpec((tm, GLA_VW), lambda i: (i, 0)),
        out_shape=jax.ShapeDtypeStruct((n, GLA_VW), BF16),
        compiler_params=_cp(("arbitrary",)),
        name="gla_post",
    )(o_f, o_b, z, norm_g.reshape(DEPTH, 1, GLA_VW))


def _att_prep_kernel(q_ref, k_ref, v_ref, cos_ref, sin_ref, qo_ref, ko_ref, vo_ref):
    cos = cos_ref[...]
    sin = sin_ref[...]
    lane = lax.broadcasted_iota(jnp.int32, (1, LANES), 1)
    first_half = (lane % ATT_HEAD_DIM) < (ATT_HEAD_DIM // 2)
    low_head = lane < ATT_HEAD_DIM

    def rope(xc):
        rot = jnp.where(first_half, pltpu.roll(xc, LANES - ATT_HEAD_DIM // 2, 1), pltpu.roll(xc, ATT_HEAD_DIM // 2, 1))
        return xc * cos + rot * sin

    def dup(xc, parity):
        sw = pltpu.roll(xc, ATT_HEAD_DIM, 1)
        return jnp.where(low_head, xc, sw) if parity == 0 else jnp.where(low_head, sw, xc)

    for cidx in range(ATT_QW // LANES):
        cs = slice(cidx * LANES, (cidx + 1) * LANES)
        qo_ref[:, cs] = (rope(q_ref[:, cs]) * (ATT_HEAD_DIM ** -0.5)).astype(BF16)
    for kh in range(ATT_KV_HEADS):
        ps = slice((kh // 2) * LANES, (kh // 2 + 1) * LANES)
        os_ = slice(kh * LANES, (kh + 1) * LANES)
        ko_ref[:, os_] = dup(rope(k_ref[:, ps]), kh % 2).astype(BF16)
        vo_ref[:, os_] = dup(v_ref[:, ps], kh % 2).astype(BF16)


def att_prep(z, cos_t, sin_t):
    n = z.shape[0]
    tm = 256
    kw = ATT_KV_HEADS * LANES
    return pl.pallas_call(
        _att_prep_kernel,
        grid=(n // tm,),
        in_specs=[pl.BlockSpec((tm, ATT_QW), lambda i: (i, OFF_AQ // ATT_QW)),
                  pl.BlockSpec((tm, ATT_KVW), lambda i: (i, OFF_AK // ATT_KVW)),
                  pl.BlockSpec((tm, ATT_KVW), lambda i: (i, OFF_AV // ATT_KVW)),
                  pl.BlockSpec((tm, LANES), lambda i: (i, 0)),
                  pl.BlockSpec((tm, LANES), lambda i: (i, 0))],
        out_specs=[pl.BlockSpec((tm, ATT_QW), lambda i: (i, 0)),
                   pl.BlockSpec((tm, kw), lambda i: (i, 0)),
                   pl.BlockSpec((tm, kw), lambda i: (i, 0))],
        out_shape=[jax.ShapeDtypeStruct((n, ATT_QW), BF16),
                   jax.ShapeDtypeStruct((n, kw), BF16),
                   jax.ShapeDtypeStruct((n, kw), BF16)],
        compiler_params=_cp(("arbitrary",)),
        name="att_prep",
    )(z, z, z, cos_t, sin_t)


def _att_heads(q_ref, k_all, v_all, bias, sink_ref, o_ref):
    lane = lax.broadcasted_iota(jnp.int32, (1, LANES), 1)
    low_head = lane < ATT_HEAD_DIM
    grp = ATT_HEADS // ATT_KV_HEADS
    blk = q_ref.shape[0]
    for kh in range(ATT_KV_HEADS):
        k2 = k_all[:, kh * LANES:(kh + 1) * LANES]
        v2 = v_all[:, kh * LANES:(kh + 1) * LANES]
        qs, sinks = [], []
        for j in range(grp):
            h = kh * grp + j
            qp = q_ref[:, (h // 2) * LANES:(h // 2 + 1) * LANES]
            qs.append(jnp.where(low_head if h % 2 == 0 else jnp.logical_not(low_head), qp, jnp.zeros_like(qp)))
            sinks.append(jnp.full((blk, 1), sink_ref[h], F32))
        s = _dot_nt(jnp.concatenate(qs, axis=0), k2)
        if bias is not None:
            s = s + jnp.concatenate([bias] * grp, axis=0)
        sk = jnp.concatenate(sinks, axis=0)
        m = jnp.maximum(jnp.max(s, axis=-1, keepdims=True), sk)
        p = jnp.exp(s - m)
        den = jnp.sum(p, axis=-1, keepdims=True) + jnp.exp(sk - m)
        o = _dot(p.astype(BF16), v2) * (1.0 / den)
        for pr in range(grp // 2):
            pair = kh * (grp // 2) + pr
            even, odd = o[(2 * pr) * blk:(2 * pr + 1) * blk], o[(2 * pr + 1) * blk:(2 * pr + 2) * blk]
            o_ref[:, pair * LANES:(pair + 1) * LANES] = jnp.where(low_head, even, odd).astype(BF16)


def _att_kernel(sink_ref, q_ref, kp_ref, kc_ref, kn_ref, kx_ref, vp_ref, vc_ref, vn_ref, vx_ref, o_ref, *, seq):
    i = pl.program_id(0)
    blk = ATT_BLOCK
    k_all = jnp.concatenate([kp_ref[...], kc_ref[...], kn_ref[...], kx_ref[...]], axis=0)
    v_all = jnp.concatenate([vp_ref[...], vc_ref[...], vn_ref[...], vx_ref[...]], axis=0)
    nctx = kx_ref.shape[0]
    qpos = i * blk + lax.broadcasted_iota(jnp.int32, (blk, 1), 0)
    kpos = (i - 1) * blk + lax.broadcasted_iota(jnp.int32, (1, 3 * blk), 1)
    valid = (jnp.abs(kpos - qpos) <= WINDOW) & (kpos >= 0) & (kpos < seq)
    bias = jnp.concatenate([jnp.where(valid, 0.0, -jnp.inf), jnp.zeros((blk, nctx), F32)], axis=1)
    _att_heads(q_ref, k_all, v_all, bias, sink_ref, o_ref)


def _att_ctx_kernel(sink_ref, q_ref, kx_ref, vx_ref, o_ref):
    _att_heads(q_ref, kx_ref[...], vx_ref[...], None, sink_ref, o_ref)


def attention(qr, k2, v2, sink, ctx_len):
    n = qr.shape[0]
    blk = ATT_BLOCK
    seq = n - ctx_len
    nb, cb = seq // blk, ctx_len // blk
    kw = ATT_KV_HEADS * LANES
    prev = lambda i, s: (jnp.maximum(i - 1, 0) + cb, 0)
    cur = lambda i, s: (i + cb, 0)
    nxt = lambda i, s: (jnp.minimum(i + 1, nb - 1) + cb, 0)
    cx = lambda i, s: (0, 0)
    band = [pl.BlockSpec((blk, kw), prev), pl.BlockSpec((blk, kw), cur), pl.BlockSpec((blk, kw), nxt),
            pl.BlockSpec((ctx_len, kw), cx)]
    y_lat = pl.pallas_call(
        functools.partial(_att_kernel, seq=seq),
        grid_spec=pltpu.PrefetchScalarGridSpec(
            num_scalar_prefetch=1, grid=(nb,),
            in_specs=[pl.BlockSpec((blk, ATT_QW), cur)] + band + band,
            out_specs=pl.BlockSpec((blk, ATT_QW), lambda i, s: (i, 0))),
        out_shape=jax.ShapeDtypeStruct((seq, ATT_QW), BF16),
        compiler_params=_cp(("arbitrary",)),
        name="att_window",
    )(sink, qr, k2, k2, k2, k2, v2, v2, v2, v2)
    y_ctx = pl.pallas_call(
        _att_ctx_kernel,
        grid_spec=pltpu.PrefetchScalarGridSpec(
            num_scalar_prefetch=1, grid=(cb,),
            in_specs=[pl.BlockSpec((blk, ATT_QW), lambda i, s: (i, 0)),
                      pl.BlockSpec((ctx_len, kw), cx), pl.BlockSpec((ctx_len, kw), cx)],
            out_specs=pl.BlockSpec((blk, ATT_QW), lambda i, s: (i, 0))),
        out_shape=jax.ShapeDtypeStruct((ctx_len, ATT_QW), BF16),
        compiler_params=_cp(("arbitrary",)),
        name="att_context",
    )(sink, qr, k2, v2)
    return jnp.concatenate([y_ctx, y_lat], axis=0)


def _merge_kernel(a1_ref, a2_ref, a3_ref, w1_ref, w2_ref, w3_ref, g1_ref, g2_ref, g3_ref, o_ref):
    acc = _sigmoid(g1_ref[...]) * _dot(a1_ref[...], w1_ref[...].astype(BF16))
    acc = acc + _sigmoid(g2_ref[...]) * _dot(a2_ref[...], w2_ref[...].astype(BF16))
    acc = acc + _sigmoid(g3_ref[...]) * _dot(a3_ref[...], w3_ref[...].astype(BF16))
    o_ref[...] = acc.astype(BF16)


def merge_branches(y_s5, y_gla, y_att, z, w_s5, w_gla, w_att, l):
    n = z.shape[0]
    tm = _pick(n, (528, 640, 256))
    tn = 512
    a_spec = pl.BlockSpec((tm, S5_WIDTH), lambda j, i: (i, 0))
    w_spec = pl.BlockSpec((None, S5_WIDTH, tn), lambda j, i: (l, 0, j))

    def g_spec(b):
        return pl.BlockSpec((tm, tn), lambda j, i: (i, (OFF_GATE + b * D_MODEL) // tn + j))

    return pl.pallas_call(
        _merge_kernel,
        grid=(D_MODEL // tn, n // tm),
        in_specs=[a_spec, a_spec, a_spec, w_spec, w_spec, w_spec, g_spec(0), g_spec(1), g_spec(2)],
        out_specs=pl.BlockSpec((tm, tn), lambda j, i: (i, j)),
        out_shape=jax.ShapeDtypeStruct((n, D_MODEL), BF16),
        compiler_params=_cp(("arbitrary", "arbitrary")),
        name="merge",
    )(y_s5, y_gla, y_att, w_s5, w_gla, w_att, z, z, z)


def _out_kernel(m_ref, w_ref, x_ref, mod_ref, g_ref, rt_ref, x1_ref, h2_ref, lg_ref, *, ctx_len, tm):
    i = pl.program_id(0)
    is_ctx = (i * tm + lax.broadcasted_iota(jnp.int32, (tm, 1), 0)) < ctx_len
    x1 = x_ref[...] + _row_mods(mod_ref, 2, is_ctx) * _dot(m_ref[...], w_ref[...])
    x1_ref[...] = x1
    r = lax.rsqrt(jnp.mean(x1 * x1, axis=-1, keepdims=True) + EPS)
    h2 = ((x1 * r) * g_ref[0]) * (1.0 + _row_mods(mod_ref, 4, is_ctx)) + _row_mods(mod_ref, 3, is_ctx)
    h_hi = h2.astype(BF16)
    h2_ref[...] = h_hi
    h_lo = (h2 - h_hi.astype(F32)).astype(BF16)
    rt = rt_ref[...]
    r_hi = rt.astype(BF16)
    r_lo = (rt - r_hi.astype(F32)).astype(BF16)
    lg_ref[...] = _dot(h_hi, r_hi) + (_dot(h_lo, r_hi) + _dot(h_hi, r_lo))


def out_projection(m, w_out_bf, xs, mods, norm_g, router_pad, l, ctx_len):
    n = xs.shape[0]
    tm = _pick(n, (528, 640, 256))
    once = pl.Buffered(1)
    return pl.pallas_call(
        functools.partial(_out_kernel, ctx_len=ctx_len, tm=tm),
        grid=(n // tm,),
        in_specs=[pl.BlockSpec((tm, D_MODEL), lambda i: (i, 0)),
                  pl.BlockSpec((None, D_MODEL, D_MODEL), lambda i: (l, 0, 0), pipeline_mode=once),
                  pl.BlockSpec((tm, D_MODEL), lambda i: (i, 0)),
                  pl.BlockSpec((None, 8, N_MOD * D_MODEL), lambda i: (l, 0, 0)),
                  pl.BlockSpec((1, 1, D_MODEL), lambda i: (l, 0, 0)),
                  pl.BlockSpec((None, D_MODEL, LANES), lambda i: (l, 0, 0), pipeline_mode=once)],
        out_specs=[pl.BlockSpec((tm, D_MODEL), lambda i: (i, 0)),
                   pl.BlockSpec((tm, D_MODEL), lambda i: (i, 0)),
                   pl.BlockSpec((tm, LANES), lambda i: (i, 0))],
        out_shape=[jax.ShapeDtypeStruct((n, D_MODEL), F32),
                   jax.ShapeDtypeStruct((n, D_MODEL), BF16),
                   jax.ShapeDtypeStruct((n, LANES), F32)],
        compiler_params=_cp(("arbitrary",)),
        name="out_projection",
    )(m, w_out_bf, xs, mods, norm_g.reshape(DEPTH, 1, D_MODEL), router_pad)


def _route_kernel(lg_ref, aff_ref, slot_ref, slot_t_ref, s0_ref, cnt_ref, *, ctx_len, n):
    tt = TOK_TILE
    lane = lax.broadcasted_iota(jnp.int32, (1, LANES), 1)
    is_expert = lane < N_EXPERTS

    def softmax_tile(b, _):
        r0 = pl.multiple_of(b * tt, tt)
        x = jnp.where(is_expert, lg_ref[pl.ds(r0, tt), :], -jnp.inf)
        e = jnp.exp(x - jnp.max(x, axis=-1, keepdims=True))
        aff_ref[pl.ds(r0, tt), :] = e / jnp.sum(e, axis=-1, keepdims=True)
        return 0

    lax.fori_loop(0, n // tt, softmax_tile, 0)

    ri = lax.broadcasted_iota(jnp.int32, (tt, tt), 0)
    ci = lax.broadcasted_iota(jnp.int32, (tt, tt), 1)
    before = jnp.where(ci < ri, 1.0, 0.0).astype(BF16)

    def bits_of(b):
        r0 = pl.multiple_of(b * tt, tt)
        return pltpu.bitcast(aff_ref[pl.ds(r0, tt), :], jnp.int32)

    def count(b0, b1, pred):
        def body(b, acc):
            return acc + jnp.sum(jnp.where(pred(bits_of(b)), 1.0, 0.0), axis=0, keepdims=True)
        return lax.fori_loop(b0, b1, body, jnp.zeros((1, LANES), F32))

    def route_set(b0, b1, slot_base):
        cap = float(EC_CAPACITY * (b1 - b0) * tt // N_EXPERTS)

        def bisect(_, lh):
            lo, hi = lh
            mid = lo + lax.shift_right_logical(hi - lo + 1, 1)
            ok = count(b0, b1, lambda v: v >= mid) >= cap
            return jnp.where(ok, mid, lo), jnp.where(ok, hi, mid - 1)

        lo0 = jnp.zeros((1, LANES), jnp.int32)
        hi0 = jnp.full((1, LANES), 0x7F800000, jnp.int32)
        thr, _ = lax.fori_loop(0, 31, bisect, (lo0, hi0))
        need = cap - count(b0, b1, lambda v: v > thr)

        def assign(b, carry):
            tie_seen, taken = carry
            r0 = pl.multiple_of(b * tt, tt)
            v = bits_of(b)
            tie = jnp.where(v == thr, 1.0, 0.0)
            tie_rank = _dot(before, tie.astype(BF16)) + tie_seen
            sel = jnp.where((v > thr) | ((v == thr) & (tie_rank < need)), 1.0, 0.0)
            rank = _dot(before, sel.astype(BF16)) + taken
            slot = jnp.where(sel > 0.0, rank + slot_base, -1.0)
            slot_ref[pl.ds(r0, tt), :] = slot.astype(jnp.int32)
            slot_t_ref[:, pl.ds(r0, tt)] = jnp.transpose(slot)[0:N_EXPERTS, :].astype(jnp.int32)
            n_sel = jnp.sum(sel, axis=0, keepdims=True)
            s0_ref[b] = (taken + slot_base).astype(jnp.int32)
            cnt_ref[b] = n_sel.astype(jnp.int32)
            return tie_seen + jnp.sum(tie, axis=0, keepdims=True), taken + n_sel

        zero = jnp.zeros((1, LANES), F32)
        lax.fori_loop(b0, b1, assign, (zero, zero))
        return cap

    cb = ctx_len // tt
    cap_ctx = route_set(0, cb, 0.0)
    route_set(cb, n // tt, cap_ctx)


def route(logits, ctx_len):
    n = logits.shape[0]
    nt = n // TOK_TILE
    full = lambda shape: pl.BlockSpec(shape, lambda i: tuple(0 for _ in shape))
    return pl.pallas_call(
        functools.partial(_route_kernel, ctx_len=ctx_len, n=n),
        grid=(1,),
        in_specs=[full((n, LANES))],
        out_specs=[full((n, LANES)), full((n, LANES)), full((N_EXPERTS, n)), full((nt, 1, LANES)),
                   full((nt, 1, LANES))],
        out_shape=[jax.ShapeDtypeStruct((n, LANES), F32),
                   jax.ShapeDtypeStruct((n, LANES), jnp.int32),
                   jax.ShapeDtypeStruct((N_EXPERTS, n), jnp.int32),
                   jax.ShapeDtypeStruct((nt, 1, LANES), jnp.int32),
                   jax.ShapeDtypeStruct((nt, 1, LANES), jnp.int32)],
        compiler_params=_cp(("arbitrary",)),
        name="route",
    )(logits)


def _window_plan(s0, cnt, rows, align_bits=3):
    a0 = lax.shift_left(lax.shift_right_logical(s0, align_bits), align_bits)
    nwin = jnp.where(cnt > 0, lax.shift_right_logical(s0 - a0 + cnt + MOE_WIN - 1, MOE_WIN.bit_length() - 1), 0)
    return a0, nwin, rows - MOE_WIN


GATHER_GROUP = 4
BF16_ROWS_LOG2 = 4


def _gather_kernel(s0_ref, cnt_ref, h_ref, slot_ref, xs_ref, *, rows):
    g = pl.program_id(0)
    t = pl.program_id(1)

    @pl.when(t == 0)
    def _():
        xs_ref[...] = jnp.zeros_like(xs_ref)

    for i in range(GATHER_GROUP):
        e = g * GATHER_GROUP + i
        a0, nwin, last = _window_plan(s0_ref[t, e], cnt_ref[t, e], rows, BF16_ROWS_LOG2)
        slot = slot_ref[i]

        def body(j, _, i=i, a0=a0, last=last, slot=slot):
            start = a0 + j * MOE_WIN
            base = pl.multiple_of(jnp.minimum(start, last), 1 << BF16_ROWS_LOG2)
            want = base + lax.broadcasted_iota(jnp.int32, (MOE_WIN, 1), 0)
            want = jnp.where(want >= start, want, -2)
            onehot = jnp.where(slot == want, 1.0, 0.0).astype(BF16)
            prev = xs_ref[i, pl.ds(base, MOE_WIN), :].astype(F32)
            xs_ref[i, pl.ds(base, MOE_WIN), :] = (prev + _dot(onehot, h_ref[...])).astype(BF16)
            return 0

        lax.fori_loop(0, nwin, body, 0)


def moe_gather(h2, slot_t, s0, cnt, rows):
    n = h2.shape[0]
    nt = n // TOK_TILE
    gg = GATHER_GROUP
    return pl.pallas_call(
        functools.partial(_gather_kernel, rows=rows),
        grid_spec=pltpu.PrefetchScalarGridSpec(
            num_scalar_prefetch=2, grid=(N_EXPERTS // gg, nt),
            in_specs=[pl.BlockSpec((TOK_TILE, D_MODEL), lambda g, t, a, b: (t, 0)),
                      pl.BlockSpec((gg, 1, TOK_TILE), lambda g, t, a, b: (g, 0, t))],
            out_specs=pl.BlockSpec((gg, rows, D_MODEL), lambda g, t, a, b: (g, 0, 0))),
        out_shape=jax.ShapeDtypeStruct((N_EXPERTS, rows, D_MODEL), BF16),
        compiler_params=_cp(("arbitrary", "arbitrary")),
        name="moe_gather",
    )(s0, cnt, h2, slot_t.reshape(N_EXPERTS, 1, n))


def _ffn_kernel(xs_ref, wg_ref, wu_ref, wd_ref, y_ref, act_scr):
    f = pl.program_id(1)
    nf = EXPERT_FF // FF_TILE

    @pl.when(f < nf)
    def _():
        x = xs_ref[0]
        a = _dot(x, wg_ref[0].astype(BF16))
        u = _dot(x, wu_ref[0].astype(BF16))
        act_scr[:, pl.ds(pl.multiple_of(f * FF_TILE, FF_TILE), FF_TILE)] = ((a * _sigmoid(a)) * u).astype(BF16)

    @pl.when(f >= nf)
    def _():
        y_ref[0] = _dot(act_scr[...], wd_ref[0].astype(BF16)).astype(BF16)


def moe_ffn(xs, w_gate, w_up, w_down, l):
    rows = xs.shape[1]
    nf = EXPERT_FF // FF_TILE
    up_idx = lambda e, f: (l, e, 0, jnp.minimum(f, nf - 1))
    return pl.pallas_call(
        _ffn_kernel,
        grid=(N_EXPERTS, nf + D_MODEL // FF_TILE),
        in_specs=[pl.BlockSpec((1, rows, D_MODEL), lambda e, f: (e, 0, 0)),
                  pl.BlockSpec((None, 1, D_MODEL, FF_TILE), up_idx),
                  pl.BlockSpec((None, 1, D_MODEL, FF_TILE), up_idx),
                  pl.BlockSpec((None, 1, EXPERT_FF, FF_TILE), lambda e, f: (l, e, 0, jnp.maximum(f - nf, 0)))],
        out_specs=pl.BlockSpec((1, rows, FF_TILE), lambda e, f: (e, 0, jnp.maximum(f - nf, 0))),
        out_shape=jax.ShapeDtypeStruct((N_EXPERTS, rows, D_MODEL), BF16),
        scratch_shapes=[pltpu.VMEM((rows, EXPERT_FF), BF16)],
        compiler_params=_cp(("arbitrary", "arbitrary")),
        name="moe_ffn",
    )(xs, w_gate, w_up, w_down)


MORE_WINS = TOK_TILE // MOE_WIN


def _combine_kernel(s0_ref, cnt_ref, x_ref, slot_ref, aff_ref, mod_ref, g_ref, y_hbm, o_ref, buf0, bufx, sem0, semx,
                    *, rows, ctx_len, final):
    t = pl.program_id(0)
    nt = pl.num_programs(0)
    tt = TOK_TILE

    def first_copy(tile, e, par):
        a0, _, last = _window_plan(s0_ref[tile, e], cnt_ref[tile, e], rows)
        base = pl.multiple_of(jnp.minimum(a0, last), 8)
        cp = pltpu.make_async_copy(y_hbm.at[e, pl.ds(base, MOE_WIN)],
                                   buf0.at[par, pl.ds(e * MOE_WIN, MOE_WIN)], sem0.at[par, e])
        return base, a0, cp

    def more_copy(e, j, a0, last):
        base = pl.multiple_of(jnp.minimum(a0 + j * MOE_WIN, last), 8)
        k = e * MORE_WINS + j - 1
        return base, pltpu.make_async_copy(y_hbm.at[e, pl.ds(base, MOE_WIN)], bufx.at[k], semx.at[k])

    @pl.when(t == 0)
    def _():
        for e in range(N_EXPERTS):
            first_copy(0, e, 0)[2].start()

    @pl.when(t + 1 < nt)
    def _():
        for e in range(N_EXPERTS):
            first_copy(t + 1, e, (t + 1) % 2)[2].start()

    for e in range(N_EXPERTS):
        a0, nwin, last = _window_plan(s0_ref[t, e], cnt_ref[t, e], rows)

        def start(j, _, e=e, a0=a0, last=last):
            more_copy(e, j, a0, last)[1].start()
            return 0

        lax.fori_loop(1, nwin, start, 0)

    par = t % 2
    lane = lax.broadcasted_iota(jnp.int32, (1, LANES), 1)
    left = lane < MOE_WIN
    offs = jnp.where(left, lane, lane - MOE_WIN)
    cols_hi, cols_lo = [], []
    for c in range(N_EXPERTS * MOE_WIN // LANES):
        e0, e1 = 2 * c, 2 * c + 1
        b0, a00, cp0 = first_copy(t, e0, par)
        b1, a01, cp1 = first_copy(t, e1, par)
        cp0.wait()
        cp1.wait()
        want = jnp.where(left, b0, b1) + offs
        want = jnp.where(want >= jnp.where(left, a00, a01), want, -2)
        picked = jnp.where(left, slot_ref[:, e0:e0 + 1], slot_ref[:, e1:e1 + 1]) == want
        gate = jnp.where(picked, jnp.where(left, aff_ref[:, e0:e0 + 1], aff_ref[:, e1:e1 + 1]), 0.0)
        g_hi = gate.astype(BF16)
        cols_hi.append(g_hi)
        cols_lo.append((gate - g_hi.astype(F32)).astype(BF16))
    y0 = buf0[par]
    o_ref[...] = _dot(jnp.concatenate(cols_hi, axis=-1), y0) + _dot(jnp.concatenate(cols_lo, axis=-1), y0)

    for e in range(N_EXPERTS):
        a0, nwin, last = _window_plan(s0_ref[t, e], cnt_ref[t, e], rows)

        def take(j, _, e=e, a0=a0, last=last):
            base, cp = more_copy(e, j, a0, last)
            cp.wait()
            want = base + lax.broadcasted_iota(jnp.int32, (1, MOE_WIN), 1)
            want = jnp.where(want >= a0 + j * MOE_WIN, want, -2)
            onehot = jnp.where(slot_ref[:, e:e + 1] == want, 1.0, 0.0).astype(BF16)
            o_ref[...] += aff_ref[:, e:e + 1] * _dot(onehot, bufx[e * MORE_WINS + j - 1])
            return 0

        lax.fori_loop(1, nwin, take, 0)

    is_ctx = (t * tt + lax.broadcasted_iota(jnp.int32, (tt, 1), 0)) < ctx_len
    x2 = x_ref[...] + _row_mods(mod_ref, 5, is_ctx) * o_ref[...]
    if final:
        r = lax.rsqrt(jnp.mean(x2 * x2, axis=-1, keepdims=True) + EPS)
        x2 = (x2 * r) * g_ref[...]
    o_ref[...] = x2


def moe_combine(x1, slot, aff, s0, cnt, mods, y, final_g, l, ctx_len, final):
    n = x1.shape[0]
    rows = y.shape[1]
    tt = TOK_TILE
    nbuf = N_EXPERTS * MORE_WINS
    return pl.pallas_call(
        functools.partial(_combine_kernel, rows=rows, ctx_len=ctx_len, final=final),
        grid_spec=pltpu.PrefetchScalarGridSpec(
            num_scalar_prefetch=2, grid=(n // tt,),
            in_specs=[pl.BlockSpec((tt, D_MODEL), lambda t, a, b: (t, 0)),
                      pl.BlockSpec((tt, LANES), lambda t, a, b: (t, 0)),
                      pl.BlockSpec((tt, LANES), lambda t, a, b: (t, 0)),
                      pl.BlockSpec((None, 8, N_MOD * D_MODEL), lambda t, a, b: (l, 0, 0)),
                      pl.BlockSpec((1, D_MODEL), lambda t, a, b: (0, 0)),
                      pl.BlockSpec(memory_space=pl.ANY)],
            out_specs=pl.BlockSpec((tt, D_MODEL), lambda t, a, b: (t, 0)),
            scratch_shapes=[pltpu.VMEM((2, N_EXPERTS * MOE_WIN, D_MODEL), BF16),
                            pltpu.VMEM((nbuf, MOE_WIN, D_MODEL), BF16),
                            pltpu.SemaphoreType.DMA((2, N_EXPERTS)),
                            pltpu.SemaphoreType.DMA((nbuf,))]),
        out_shape=jax.ShapeDtypeStruct((n, D_MODEL), F32),
        compiler_params=_cp(("arbitrary",)),
        name="moe_combine",
    )(s0, cnt, x1, slot, aff, mods, final_g.reshape(1, D_MODEL), y)


def rope_tables(seq, ctx_len):
    pos = jnp.arange(seq)
    row = (pos // GRID_W).astype(F32)
    col = (pos % GRID_W).astype(F32)
    n_freq = ATT_HEAD_DIM // 4
    inv_freq = ROPE_BASE ** (-jnp.arange(n_freq, dtype=F32) / n_freq)
    ang = jnp.concatenate([row[:, None] * inv_freq, col[:, None] * inv_freq], axis=-1)
    cos, sin = jnp.cos(ang), jnp.sin(ang)
    cos_t = jnp.concatenate([jnp.ones((ctx_len, LANES), F32), jnp.tile(cos, (1, 4))], axis=0)
    sin_t = jnp.concatenate([jnp.zeros((ctx_len, LANES), F32), jnp.tile(jnp.concatenate([-sin, sin], -1), (1, 2))],
                            axis=0)
    return cos_t, sin_t


def kernel(x, c, ctx, c_ctx, ada_w, ada_b, norm1_g, norm2_g, w_in, s5_lam_re, s5_lam_im, s5_log_dt,
           s5_b_re, s5_b_im, s5_c_re, s5_c_im, s5_d, s5_w_glu, gla_w1, gla_w2, gla_b, gla_norm_g,
           attn_sink, w_branch_s5, w_branch_gla, w_branch_attn, w_out, moe_router, moe_w_gate,
           moe_w_up, moe_w_down, final_g):
    seq, ctx_len = x.shape[1], ctx.shape[1]
    n = seq + ctx_len
    assert x.shape[0] == 1 and ctx_len % TOK_TILE == 0 and seq % TOK_TILE == 0
    xs = jnp.concatenate([ctx[0], x[0]], axis=0)
    c8 = jnp.zeros((8, D_MODEL), F32).at[0].set(c[0]).at[1].set(c_ctx)
    mods = ada_mods(c8, ada_w, ada_b)
    cos_t, sin_t = rope_tables(seq, ctx_len)
    w_out_bf = w_out.astype(BF16)
    router_pad = jnp.pad(moe_router, ((0, 0), (0, 0), (0, LANES - N_EXPERTS)))
    rows = EC_CAPACITY * n // N_EXPERTS
    for l in range(DEPTH):
        w1cat = jnp.pad(jnp.concatenate([gla_w1[l, 0], gla_w1[l, 1]], axis=-1),
                        ((0, 0), (0, LANES - 2 * GLA_RANK)))
        w2pad = jnp.zeros((2, LANES, GLA_KW), F32)
        w2pad = w2pad.at[0, 0:GLA_RANK].set(gla_w2[l, 0]).at[1, GLA_RANK:2 * GLA_RANK].set(gla_w2[l, 1])
        z, lr = in_projection(xs, norm1_g, mods, w_in, w1cat, l, ctx_len)
        s5p = s5_params(s5_lam_re[l], s5_lam_im[l], s5_log_dt[l], s5_b_re[l], s5_b_im[l])
        y_s5 = s5_glu(s5_mix(z, s5p, s5_c_re[l], s5_c_im[l], s5_d[l], ctx_len), s5_w_glu, l)
        gla_bias = gla_b[l].reshape(2, 1, GLA_KW)
        o_f = gla_direction(z, lr, w2pad, gla_bias, 0, ctx_len)
        o_b = gla_direction(z, lr, w2pad, gla_bias, 1, ctx_len)
        y_gla = gla_post(o_f, o_b, z, gla_norm_g, l)
        qr, k2, v2 = att_prep(z, cos_t, sin_t)
        y_att = attention(qr, k2, v2, attn_sink[l], ctx_len)
        m = merge_branches(y_s5, y_gla, y_att, z, w_branch_s5, w_branch_gla, w_branch_attn, l)
        x1, h2, logits = out_projection(m, w_out_bf, xs, mods, norm2_g, router_pad, l, ctx_len)
        aff, slot, slot_t, s0, cnt = route(logits, ctx_len)
        s0, cnt = s0[:, 0, :N_EXPERTS], cnt[:, 0, :N_EXPERTS]
        xe = moe_gather(h2, slot_t, s0, cnt, rows)
        ye = moe_ffn(xe, moe_w_gate, moe_w_up, moe_w_down, l)
        xs = moe_combine(x1, slot, aff, s0, cnt, mods, ye, final_g, l, ctx_len, l == DEPTH - 1)
    return xs[ctx_len:][None]
```

```python
import functools
import math

import jax
import jax.numpy as jnp
from jax import lax
from jax.experimental import pallas as pl
from jax.experimental.pallas import tpu as pltpu

D_MODEL = 2048
DEPTH = 2
GRID_W = 64
EPS = 1e-6
N_MOD = 6
S5_WIDTH = 1024
S5_GROUP_CH = 16
S5_GROUPS = S5_WIDTH // S5_GROUP_CH
S5_STATE = 64
GLA_HEADS = 4
GLA_DK = 128
GLA_DV = 256
GLA_KW = GLA_HEADS * GLA_DK
GLA_VW = GLA_HEADS * GLA_DV
GLA_RANK = 16
GLA_TAU = 16.0
GLA_CHUNK = 64
ATT_HEADS = 16
ATT_KV_HEADS = 4
ATT_HEAD_DIM = 64
ATT_QW = ATT_HEADS * ATT_HEAD_DIM
ATT_KVW = ATT_KV_HEADS * ATT_HEAD_DIM
WINDOW = 128
ATT_BLOCK = 128
ROPE_BASE = 10000.0
N_BRANCH = 3
IN_SPLITS = (S5_WIDTH, GLA_KW, GLA_KW, GLA_VW, GLA_VW, ATT_QW, ATT_KVW, ATT_KVW, N_BRANCH * D_MODEL)
IN_WIDTH = sum(IN_SPLITS)
N_EXPERTS = 16
EXPERT_FF = 2048
EC_CAPACITY = 2

OFF_GQ = 0
OFF_GK = OFF_GQ + GLA_KW
OFF_GV = OFF_GK + GLA_KW
OFF_GR = OFF_GV + GLA_VW
OFF_AQ = OFF_GR + GLA_VW
OFF_AK = OFF_AQ + ATT_QW
OFF_AV = OFF_AK + ATT_KVW
OFF_GATE = OFF_AV + ATT_KVW

LANES = 128
S5_T = 16
S5_PAIRS = S5_GROUPS // 2
S5_BLOCKS = S5_WIDTH // LANES
S5_BLK_GROUPS = LANES // S5_GROUP_CH
S5_BLK_PAIRS = S5_BLK_GROUPS // 2
S5_BLK_STATE = S5_BLK_GROUPS * S5_STATE
TOK_TILE = 256
MOE_WIN = 64
FF_TILE = 512
VMEM_MB = 56

F32 = jnp.float32
BF16 = jnp.bfloat16
HI = lax.Precision.HIGHEST


def _cp(sem, mb=VMEM_MB):
    return pltpu.CompilerParams(dimension_semantics=sem, vmem_limit_bytes=mb << 20)


def _dot(a, b):
    return jnp.dot(a, b, preferred_element_type=F32)


def _dot_nt(a, b):
    return lax.dot_general(a, b, (((1,), (1,)), ((), ())), preferred_element_type=F32)


def _dot_tn(a, b):
    return lax.dot_general(a, b, (((0,), (0,)), ((), ())), preferred_element_type=F32)


def _pick(n, cands):
    for c in cands:
        if n % c == 0:
            return c
    raise ValueError(f"no tile for {n} in {cands}")


def _sigmoid(x):
    return 1.0 / (1.0 + jnp.exp(-x))


def _row_mods(mod_ref, k, is_ctx):
    lo, hi = k * D_MODEL, (k + 1) * D_MODEL
    return jnp.where(is_ctx, mod_ref[1:2, lo:hi], mod_ref[0:1, lo:hi])


def _ada_kernel(c_ref, w_ref, b_ref, o_ref):
    cc = c_ref[...]
    o_ref[0] = jnp.dot(cc * _sigmoid(cc), w_ref[0], preferred_element_type=F32, precision=HI) + b_ref[0]


def ada_mods(c8, ada_w, ada_b):
    tn = 1024
    width = N_MOD * D_MODEL
    return pl.pallas_call(
        _ada_kernel,
        grid=(DEPTH, width // tn),
        in_specs=[pl.BlockSpec((8, D_MODEL), lambda l, j: (0, 0)),
                  pl.BlockSpec((1, D_MODEL, tn), lambda l, j: (l, 0, j)),
                  pl.BlockSpec((1, 1, tn), lambda l, j: (l, 0, j))],
        out_specs=pl.BlockSpec((1, 8, tn), lambda l, j: (l, 0, j)),
        out_shape=jax.ShapeDtypeStruct((DEPTH, 8, width), F32),
        compiler_params=_cp(("arbitrary", "arbitrary")),
        name="ada_mods",
    )(c8, ada_w, ada_b.reshape(DEPTH, 1, width))


def _in_kernel(x_ref, g_ref, mod_ref, w_ref, w1_ref, zu_ref, zr_ref, lr_ref, h_scr, *, ctx_len, tm, nu):
    i = pl.program_id(0)
    j = pl.program_id(1)

    @pl.when(j == 0)
    def _():
        x = x_ref[...]
        r = lax.rsqrt(jnp.mean(x * x, axis=-1, keepdims=True) + EPS)
        hn = (x * r) * g_ref[0]
        is_ctx = (i * tm + lax.broadcasted_iota(jnp.int32, (tm, 1), 0)) < ctx_len
        hb = (hn * (1.0 + _row_mods(mod_ref, 1, is_ctx)) + _row_mods(mod_ref, 0, is_ctx)).astype(BF16)
        h_scr[...] = hb
        lr_ref[...] = _dot(hb, w1_ref[...].astype(BF16))

    z = _dot(h_scr[...], w_ref[...].astype(BF16))

    @pl.when(j < nu)
    def _():
        zu_ref[...] = z

    @pl.when(j >= nu)
    def _():
        zr_ref[...] = z.astype(BF16)


def in_projection(xs, norm_g, mods, w_in, w1cat, l, ctx_len):
    n = xs.shape[0]
    tm = _pick(n, (1056, 640, 256))
    tn = 512
    nu = S5_WIDTH // tn
    return pl.pallas_call(
        functools.partial(_in_kernel, ctx_len=ctx_len, tm=tm, nu=nu),
        grid=(n // tm, IN_WIDTH // tn),
        in_specs=[pl.BlockSpec((tm, D_MODEL), lambda i, j: (i, 0)),
                  pl.BlockSpec((1, 1, D_MODEL), lambda i, j: (l, 0, 0)),
                  pl.BlockSpec((None, 8, N_MOD * D_MODEL), lambda i, j: (l, 0, 0)),
                  pl.BlockSpec((None, D_MODEL, tn), lambda i, j: (l, 0, j)),
                  pl.BlockSpec((D_MODEL, LANES), lambda i, j: (0, 0))],
        out_specs=[pl.BlockSpec((tm, tn), lambda i, j: (i, jnp.minimum(j, nu - 1))),
                   pl.BlockSpec((tm, tn), lambda i, j: (i, jnp.maximum(j - nu, 0))),
                   pl.BlockSpec((tm, LANES), lambda i, j: (i, 0))],
        out_shape=[jax.ShapeDtypeStruct((n, S5_WIDTH), F32),
                   jax.ShapeDtypeStruct((n, IN_WIDTH - S5_WIDTH), BF16),
                   jax.ShapeDtypeStruct((n, LANES), F32)],
        scratch_shapes=[pltpu.VMEM((tm, D_MODEL), BF16)],
        compiler_params=_cp(("arbitrary", "arbitrary")),
        name="in_projection",
    )(xs, norm_g.reshape(DEPTH, 1, D_MODEL), mods, w_in, w1cat)


def s5_params(lam_re, lam_im, log_dt, b_re, b_im):
    dt = jnp.exp(log_dt)[..., None]
    mag = jnp.exp(lam_re * dt)
    ab_re = mag * jnp.cos(lam_im * dt)
    ab_im = mag * jnp.sin(lam_im * dt)
    den = lam_re * lam_re + lam_im * lam_im
    nr = ab_re - 1
    f_re = ((nr * lam_re + ab_im * lam_im) / den)[:, :, None, :]
    f_im = ((ab_im * lam_re - nr * lam_im) / den)[:, :, None, :]
    bt_re, bt_im = b_re.transpose(0, 1, 3, 2), b_im.transpose(0, 1, 3, 2)
    bb_re = f_re * bt_re - f_im * bt_im
    bb_im = f_re * bt_im + f_im * bt_re
    tau = jnp.arange(S5_T + 1, dtype=F32)[None, :, None, None]
    pmag = jnp.exp((lam_re * dt)[:, None] * tau)
    ang = (lam_im * dt)[:, None] * tau
    return pmag * jnp.cos(ang), pmag * jnp.sin(ang), bb_re, bb_im


def _s5_chunk_rows(z_ref, nch):
    return jnp.concatenate([z_ref[pl.ds(t, nch, stride=S5_T), :].astype(BF16) for t in range(S5_T)], axis=-1)


def _s5_place(stage, pr, pi, wre_ref, wim_ref, d, im_sign):
    ch, n_st = S5_GROUP_CH, S5_STATE
    for g in range(S5_BLK_GROUPS):
        a_r, a_i = pr[g:g + 1, :], pi[g:g + 1, :]
        w_r, w_i = wre_ref[d, g], wim_ref[d, g]
        stage[g * ch:(g + 1) * ch, g * n_st:(g + 1) * n_st] = a_r * w_r - a_i * w_i
        stage[g * ch:(g + 1) * ch, S5_BLK_STATE + g * n_st:S5_BLK_STATE + (g + 1) * n_st] = (
            im_sign * (a_r * w_i + a_i * w_r))


def _s5_state_in_kernel(z_ref, pre_ref, pim_ref, bre_ref, bim_ref, vre_ref, vim_ref, wq_scr, stage, *, nch):
    @pl.when((pl.program_id(0) == 0) & (pl.program_id(1) == 0))
    def _():
        stage[...] = jnp.zeros_like(stage)

    rev = pl.program_id(1) == 1
    for t in range(S5_T):
        e_t = jnp.where(rev, t, S5_T - 1 - t)
        _s5_place(stage, pre_ref[0, e_t], pim_ref[0, e_t], bre_ref, bim_ref, 0, 1.0)
        wq_scr[t * LANES:(t + 1) * LANES, :] = stage[...].astype(BF16)
    v = _dot(_s5_chunk_rows(z_ref, nch), wq_scr[...])
    vre_ref[0] = v[:, :S5_BLK_STATE]
    vim_ref[0] = v[:, S5_BLK_STATE:]


def _s5_scan_kernel(vre_ref, vim_ref, are_ref, aim_ref, sre_ref, sim_ref, *, nch, ncc):
    rev = pl.program_id(0) == 1
    ar = are_ref[0]
    ai = aim_ref[0]

    def body(i, carry):
        sr, si = carry
        k = jnp.where(rev, jnp.where(i < ncc, ncc - 1 - i, nch + ncc - 1 - i), i)
        sre_ref[0, k] = sr
        sim_ref[0, k] = si
        return ar * sr - ai * si + vre_ref[0, k], ar * si + ai * sr + vim_ref[0, k]

    zero = jnp.zeros((S5_BLOCKS, LANES), F32)
    lax.fori_loop(0, nch, body, (zero, zero))


def _s5_out_kernel(z_ref, sre_ref, sim_ref, pre_ref, pim_ref, bre_ref, bim_ref, cre_ref, cim_ref, d_ref, y_ref,
                   wq_scr, wt_scr, stage, bb_scr, *, nch):
    @pl.when(pl.program_id(0) == 0)
    def _():
        stage[...] = jnp.zeros_like(stage)
        bb_scr[...] = jnp.zeros_like(bb_scr)

    one = jnp.ones((1, S5_STATE), F32)
    zero = jnp.zeros((1, S5_STATE), F32)
    ones_re = jnp.concatenate([one] * S5_BLK_GROUPS, axis=0)
    zeros_im = jnp.concatenate([zero] * S5_BLK_GROUPS, axis=0)
    for d in range(2):
        _s5_place(bb_scr.at[d], ones_re, zeros_im, bre_ref, bim_ref, d, 1.0)
        bb = bb_scr[d]
        bb_hi = bb.astype(BF16)
        bb_lo = (bb - bb_hi.astype(F32)).astype(BF16)
        for tau in range(S5_T + 1):
            _s5_place(stage, pre_ref[d, tau], pim_ref[d, tau], cre_ref, cim_ref, d, -1.0)
            blk = stage[...]
            blk_hi = blk.astype(BF16)
            if tau < S5_T:
                k = S5_T - 1 - tau if d == 0 else tau
                blk_lo = (blk - blk_hi.astype(F32)).astype(BF16)
                kq = _dot_nt(bb_hi, blk_hi) + (_dot_nt(bb_lo, blk_hi) + _dot_nt(bb_hi, blk_lo))
                wt_scr[d, k * LANES:(k + 1) * LANES, :] = kq.astype(BF16)
            if tau > 0:
                t = tau - 1 if d == 0 else S5_T - tau
                wq_scr[d, t * LANES:(t + 1) * LANES, :] = blk_hi
    rows = _s5_chunk_rows(z_ref, nch)
    carried = None
    for d in range(2):
        sp = jnp.concatenate([sre_ref[d], sim_ref[d]], axis=-1).astype(BF16)
        part = _dot_nt(sp, wq_scr[d])
        carried = part if carried is None else carried + part
    for t in range(S5_T):
        acc = z_ref[pl.ds(t, nch, stride=S5_T), :] * d_ref[...] + carried[:, t * LANES:(t + 1) * LANES]
        acc = acc + _dot(rows[:, :LANES * (t + 1)], wt_scr[0, LANES * (S5_T - 1 - t):, :])
        acc = acc + _dot(rows[:, LANES * t:], wt_scr[1, :LANES * (S5_T - t), :])
        y_ref[pl.ds(t, nch, stride=S5_T), :] = acc


def s5_mix(z, params, c_re, c_im, d_skip, ctx_len):
    n = z.shape[0]
    nch, ncc = n // S5_T, ctx_len // S5_T
    p_re, p_im, bb_re, bb_im = params
    st, tl = S5_BLK_STATE, S5_T * LANES
    gb, ch, n_st = S5_BLK_GROUPS, S5_GROUP_CH, S5_STATE
    a_re = p_re[:, S5_T].reshape(2, S5_BLOCKS, st)
    a_im = p_im[:, S5_T].reshape(2, S5_BLOCKS, st)
    v_shape = jax.ShapeDtypeStruct((2, nch, S5_BLOCKS * st), F32)
    pow_spec = pl.BlockSpec((1, S5_T + 1, gb, n_st), lambda q, d: (d, 0, q, 0))
    par_spec = pl.BlockSpec((1, gb, ch, n_st), lambda q, d: (d, q, 0, 0))
    v_re, v_im = pl.pallas_call(
        functools.partial(_s5_state_in_kernel, nch=nch),
        grid=(S5_BLOCKS, 2),
        in_specs=[pl.BlockSpec((n, LANES), lambda q, d: (0, q)), pow_spec, pow_spec, par_spec, par_spec],
        out_specs=[pl.BlockSpec((1, nch, st), lambda q, d: (d, 0, q))] * 2,
        out_shape=[v_shape, v_shape],
        scratch_shapes=[pltpu.VMEM((tl, 2 * st), BF16), pltpu.VMEM((LANES, 2 * st), F32)],
        compiler_params=_cp(("arbitrary", "arbitrary")),
        name="s5_state_in",
    )(z, p_re, p_im, bb_re, bb_im)
    blocked = (2, nch, S5_BLOCKS, st)
    v_spec = pl.BlockSpec((1, nch, S5_BLOCKS, LANES), lambda d, j: (d, 0, 0, j))
    a_spec = pl.BlockSpec((1, S5_BLOCKS, LANES), lambda d, j: (d, 0, j))
    s_shape = jax.ShapeDtypeStruct(blocked, F32)
    s_re, s_im = pl.pallas_call(
        functools.partial(_s5_scan_kernel, nch=nch, ncc=ncc),
        grid=(2, st // LANES),
        in_specs=[v_spec, v_spec, a_spec, a_spec],
        out_specs=[v_spec, v_spec],
        out_shape=[s_shape, s_shape],
        compiler_params=_cp(("arbitrary", "arbitrary")),
        name="s5_scan",
    )(v_re.reshape(blocked), v_im.reshape(blocked), a_re, a_im)
    s_spec = pl.BlockSpec((2, nch, st), lambda q: (0, 0, q))
    pow2_spec = pl.BlockSpec((2, S5_T + 1, gb, n_st), lambda q: (0, 0, q, 0))
    par2_spec = pl.BlockSpec((2, gb, ch, n_st), lambda q: (0, q, 0, 0))
    return pl.pallas_call(
        functools.partial(_s5_out_kernel, nch=nch),
        grid=(S5_BLOCKS,),
        in_specs=[pl.BlockSpec((n, LANES), lambda q: (0, q)), s_spec, s_spec, pow2_spec, pow2_spec,
                  par2_spec, par2_spec, par2_spec, par2_spec, pl.BlockSpec((1, LANES), lambda q: (0, q))],
        out_specs=pl.BlockSpec((n, LANES), lambda q: (0, q)),
        out_shape=jax.ShapeDtypeStruct((n, S5_WIDTH), F32),
        scratch_shapes=[pltpu.VMEM((2, tl, 2 * st), BF16), pltpu.VMEM((2, tl, LANES), BF16),
                        pltpu.VMEM((LANES, 2 * st), F32), pltpu.VMEM((2, LANES, 2 * st), F32)],
        compiler_params=_cp(("arbitrary",)),
        name="s5_out",
    )(z, s_re.reshape(2, nch, S5_BLOCKS * st), s_im.reshape(2, nch, S5_BLOCKS * st), p_re, p_im,
      bb_re, bb_im, c_re, c_im, d_skip.reshape(1, S5_WIDTH))


def _s5_glu_kernel(y_ref, w_ref, o_ref):
    y = y_ref[...]
    g = 0.5 * y * (1.0 + jnp.tanh(math.sqrt(2.0 / math.pi) * (y + 0.044715 * (y * y * y))))
    o_ref[...] = (g * _sigmoid(_dot(g.astype(BF16), w_ref[...].astype(BF16)))).astype(BF16)


def s5_glu(y, w_glu, l):
    n = y.shape[0]
    tm = _pick(n, (1056, 640, 256))
    return pl.pallas_call(
        _s5_glu_kernel,
        grid=(n // tm,),
        in_specs=[pl.BlockSpec((tm, S5_WIDTH), lambda i: (i, 0)),
                  pl.BlockSpec((None, S5_WIDTH, S5_WIDTH), lambda i: (l, 0, 0))],
        out_specs=pl.BlockSpec((tm, S5_WIDTH), lambda i: (i, 0)),
        out_shape=jax.ShapeDtypeStruct((n, S5_WIDTH), BF16),
        compiler_params=_cp(("arbitrary",)),
        name="s5_glu",
    )(y, w_glu)


def _gla_kernel(qf_ref, kf_ref, vf_ref, lrf_ref, qb_ref, kb_ref, vb_ref, lrb_ref, w2_ref, b_ref, of_ref, ob_ref,
                st_scr):
    @pl.when(pl.program_id(0) == 0)
    def _():
        st_scr[...] = jnp.zeros_like(st_scr)

    _gla_chunk(qf_ref, kf_ref, vf_ref, lrf_ref, w2_ref.at[0], b_ref.at[0], of_ref, st_scr.at[0], False)
    _gla_chunk(qb_ref, kb_ref, vb_ref, lrb_ref, w2_ref.at[1], b_ref.at[1], ob_ref, st_scr.at[1], True)


def _gla_chunk(q_ref, k_ref, v_ref, lr_ref, w2_ref, b_ref, o_ref, st_scr, rev):
    c = GLA_CHUNK
    x = _dot(lr_ref[...].astype(BF16), w2_ref[...].astype(BF16)) + b_ref[...]
    la = (jnp.minimum(x, 0.0) - jnp.log1p(jnp.exp(-jnp.abs(x)))) * (1.0 / GLA_TAU)
    ri = lax.broadcasted_iota(jnp.int32, (c, c), 0)
    ci = lax.broadcasted_iota(jnp.int32, (c, c), 1)
    keep = (ci >= ri) if rev else (ci <= ri)
    tri = jnp.where(keep, 1.0, 0.0).astype(BF16)
    hi = la.astype(BF16)
    r1 = la - hi.astype(F32)
    mid = r1.astype(BF16)
    lo = (r1 - mid.astype(F32)).astype(BF16)
    bc = _dot(tri, hi) + _dot(tri, mid) + _dot(tri, lo)
    last, ref_row = (0, c // 2) if rev else (c - 1, c // 2 - 1)
    btot = bc[last:last + 1]
    bmid = bc[ref_row:ref_row + 1]
    qs = q_ref[...].astype(F32) * (GLA_DK ** -0.5)
    kk = k_ref[...].astype(F32)
    q_mid = (qs * jnp.exp(bc - bmid)).astype(BF16)
    k_mid = (kk * jnp.exp(bmid - bc)).astype(BF16)
    q_in = (qs * jnp.exp(bc)).astype(BF16)
    k_out = (kk * jnp.exp(btot - bc)).astype(BF16)
    decay = jnp.exp(btot)
    for h in range(GLA_HEADS):
        ks = slice(h * GLA_DK, (h + 1) * GLA_DK)
        vs = slice(h * GLA_DV, (h + 1) * GLA_DV)
        vb = v_ref[:, vs].astype(BF16)
        sc = jnp.where(keep, _dot_nt(q_mid[:, ks], k_mid[:, ks]), 0.0).astype(BF16)
        st = st_scr[h]
        o_ref[:, vs] = _dot(sc, vb) + _dot_nt(q_in[:, ks], st.astype(BF16))
        st_scr[h] = decay[:, ks] * st + _dot_tn(vb, k_out[:, ks])


def gla_mix(z, lr, w2pad, bias, ctx_len):
    n = z.shape[0]
    c = GLA_CHUNK
    nch, ncc = n // c, ctx_len // c
    fwd = lambda i: i
    bwd = lambda i: jnp.where(i < ncc, ncc - 1 - i, nch + ncc - 1 - i)

    def chunk_specs(row):
        return [pl.BlockSpec((c, GLA_KW), lambda i: (row(i), OFF_GQ // GLA_KW)),
                pl.BlockSpec((c, GLA_KW), lambda i: (row(i), OFF_GK // GLA_KW)),
                pl.BlockSpec((c, GLA_VW), lambda i: (row(i), OFF_GV // GLA_VW)),
                pl.BlockSpec((c, LANES), lambda i: (row(i), 0))]

    o_shape = jax.ShapeDtypeStruct((n, GLA_VW), F32)
    return pl.pallas_call(
        _gla_kernel,
        grid=(nch,),
        in_specs=chunk_specs(fwd) + chunk_specs(bwd) + [
            pl.BlockSpec((2, LANES, GLA_KW), lambda i: (0, 0, 0)),
            pl.BlockSpec((2, 1, GLA_KW), lambda i: (0, 0, 0))],
        out_specs=[pl.BlockSpec((c, GLA_VW), lambda i: (fwd(i), 0)),
                   pl.BlockSpec((c, GLA_VW), lambda i: (bwd(i), 0))],
        out_shape=[o_shape, o_shape],
        scratch_shapes=[pltpu.VMEM((2, GLA_HEADS, GLA_DV, GLA_DK), F32)],
        compiler_params=_cp(("arbitrary",)),
        name="gla_mix",
    )(z, z, z, lr, z, z, z, lr, w2pad, bias)


def _gla_post_kernel(of_ref, ob_ref, r_ref, g_ref, y_ref):
    for h in range(GLA_HEADS):
        vs = slice(h * GLA_DV, (h + 1) * GLA_DV)
        o = of_ref[:, vs] + ob_ref[:, vs]
        rn = lax.rsqrt(jnp.mean(o * o, axis=-1, keepdims=True) + EPS)
        r = r_ref[:, vs].astype(F32)
        y_ref[:, vs] = ((o * rn) * g_ref[0, :, vs] * (r * _sigmoid(r))).astype(BF16)


def gla_post(o_f, o_b, z, norm_g, l):
    n = z.shape[0]
    tm = _pick(n, (1056, 640, 256))
    return pl.pallas_call(
        _gla_post_kernel,
        grid=(n // tm,),
        in_specs=[pl.BlockSpec((tm, GLA_VW), lambda i: (i, 0)),
                  pl.BlockSpec((tm, GLA_VW), lambda i: (i, 0)),
                  pl.BlockSpec((tm, GLA_VW), lambda i: (i, OFF_GR // GLA_VW)),
                  pl.BlockSpec((1, 1, GLA_VW), lambda i: (l, 0, 0))],
        out_specs=pl.BlockSpec((tm, GLA_VW), lambda i: (i, 0)),
        out_shape=jax.ShapeDtypeStruct((n, GLA_VW), BF16),
        compiler_params=_cp(("arbitrary",)),
        name="gla_post",
    )(o_f, o_b, z, norm_g.reshape(DEPTH, 1, GLA_VW))


def _att_prep_kernel(q_ref, k_ref, v_ref, cos_ref, sin_ref, qo_ref, ko_ref, vo_ref):
    cos = cos_ref[...]
    sin = sin_ref[...]
    lane = lax.broadcasted_iota(jnp.int32, (1, LANES), 1)
    first_half = (lane % ATT_HEAD_DIM) < (ATT_HEAD_DIM // 2)
    low_head = lane < ATT_HEAD_DIM

    def rope(xc):
        rot = jnp.where(first_half, pltpu.roll(xc, LANES - ATT_HEAD_DIM // 2, 1), pltpu.roll(xc, ATT_HEAD_DIM // 2, 1))
        return xc * cos + rot * sin

    def dup(xc, parity):
        sw = pltpu.roll(xc, ATT_HEAD_DIM, 1)
        return jnp.where(low_head, xc, sw) if parity == 0 else jnp.where(low_head, sw, xc)

    for cidx in range(ATT_QW // LANES):
        cs = slice(cidx * LANES, (cidx + 1) * LANES)
        qo_ref[:, cs] = (rope(q_ref[:, cs].astype(F32)) * (ATT_HEAD_DIM ** -0.5)).astype(BF16)
    for kh in range(ATT_KV_HEADS):
        ps = slice((kh // 2) * LANES, (kh // 2 + 1) * LANES)
        os_ = slice(kh * LANES, (kh + 1) * LANES)
        ko_ref[:, os_] = dup(rope(k_ref[:, ps].astype(F32)), kh % 2).astype(BF16)
        vo_ref[:, os_] = dup(v_ref[:, ps].astype(F32), kh % 2).astype(BF16)


def att_prep(z, cos_t, sin_t):
    n = z.shape[0]
    tm = 256
    kw = ATT_KV_HEADS * LANES
    return pl.pallas_call(
        _att_prep_kernel,
        grid=(n // tm,),
        in_specs=[pl.BlockSpec((tm, ATT_QW), lambda i: (i, OFF_AQ // ATT_QW)),
                  pl.BlockSpec((tm, ATT_KVW), lambda i: (i, OFF_AK // ATT_KVW)),
                  pl.BlockSpec((tm, ATT_KVW), lambda i: (i, OFF_AV // ATT_KVW)),
                  pl.BlockSpec((tm, LANES), lambda i: (i, 0)),
                  pl.BlockSpec((tm, LANES), lambda i: (i, 0))],
        out_specs=[pl.BlockSpec((tm, ATT_QW), lambda i: (i, 0)),
                   pl.BlockSpec((tm, kw), lambda i: (i, 0)),
                   pl.BlockSpec((tm, kw), lambda i: (i, 0))],
        out_shape=[jax.ShapeDtypeStruct((n, ATT_QW), BF16),
                   jax.ShapeDtypeStruct((n, kw), BF16),
                   jax.ShapeDtypeStruct((n, kw), BF16)],
        compiler_params=_cp(("arbitrary",)),
        name="att_prep",
    )(z, z, z, cos_t, sin_t)


def _att_heads(q_ref, k_all, v_all, bias, sink_ref, o_ref):
    lane = lax.broadcasted_iota(jnp.int32, (1, LANES), 1)
    low_head = lane < ATT_HEAD_DIM
    grp = ATT_HEADS // ATT_KV_HEADS
    blk = q_ref.shape[0]
    for kh in range(ATT_KV_HEADS):
        k2 = k_all[:, kh * LANES:(kh + 1) * LANES]
        v2 = v_all[:, kh * LANES:(kh + 1) * LANES]
        qs, sinks = [], []
        for j in range(grp):
            h = kh * grp + j
            qp = q_ref[:, (h // 2) * LANES:(h // 2 + 1) * LANES]
            qs.append(jnp.where(low_head if h % 2 == 0 else jnp.logical_not(low_head), qp, jnp.zeros_like(qp)))
            sinks.append(jnp.full((blk, 1), sink_ref[h], F32))
        s = _dot_nt(jnp.concatenate(qs, axis=0), k2)
        if bias is not None:
            s = s + jnp.concatenate([bias] * grp, axis=0)
        sk = jnp.concatenate(sinks, axis=0)
        m = jnp.maximum(jnp.max(s, axis=-1, keepdims=True), sk)
        p = jnp.exp(s - m)
        den = jnp.sum(p, axis=-1, keepdims=True) + jnp.exp(sk - m)
        o = _dot(p.astype(BF16), v2) * (1.0 / den)
        for pr in range(grp // 2):
            pair = kh * (grp // 2) + pr
            even, odd = o[(2 * pr) * blk:(2 * pr + 1) * blk], o[(2 * pr + 1) * blk:(2 * pr + 2) * blk]
            o_ref[:, pair * LANES:(pair + 1) * LANES] = jnp.where(low_head, even, odd).astype(BF16)


def _att_kernel(sink_ref, q_ref, kp_ref, kc_ref, kn_ref, kx_ref, vp_ref, vc_ref, vn_ref, vx_ref, o_ref, *, seq):
    i = pl.program_id(0)
    blk = ATT_BLOCK
    k_all = jnp.concatenate([kp_ref[...], kc_ref[...], kn_ref[...], kx_ref[...]], axis=0)
    v_all = jnp.concatenate([vp_ref[...], vc_ref[...], vn_ref[...], vx_ref[...]], axis=0)
    nctx = kx_ref.shape[0]
    qpos = i * blk + lax.broadcasted_iota(jnp.int32, (blk, 1), 0)
    kpos = (i - 1) * blk + lax.broadcasted_iota(jnp.int32, (1, 3 * blk), 1)
    valid = (jnp.abs(kpos - qpos) <= WINDOW) & (kpos >= 0) & (kpos < seq)
    bias = jnp.concatenate([jnp.where(valid, 0.0, -jnp.inf), jnp.zeros((blk, nctx), F32)], axis=1)
    _att_heads(q_ref, k_all, v_all, bias, sink_ref, o_ref)


def _att_ctx_kernel(sink_ref, q_ref, kx_ref, vx_ref, o_ref):
    _att_heads(q_ref, kx_ref[...], vx_ref[...], None, sink_ref, o_ref)


def attention(qr, k2, v2, sink, ctx_len):
    n = qr.shape[0]
    blk = ATT_BLOCK
    seq = n - ctx_len
    nb, cb = seq // blk, ctx_len // blk
    kw = ATT_KV_HEADS * LANES
    prev = lambda i, s: (jnp.maximum(i - 1, 0) + cb, 0)
    cur = lambda i, s: (i + cb, 0)
    nxt = lambda i, s: (jnp.minimum(i + 1, nb - 1) + cb, 0)
    cx = lambda i, s: (0, 0)
    band = [pl.BlockSpec((blk, kw), prev), pl.BlockSpec((blk, kw), cur), pl.BlockSpec((blk, kw), nxt),
            pl.BlockSpec((ctx_len, kw), cx)]
    y_lat = pl.pallas_call(
        functools.partial(_att_kernel, seq=seq),
        grid_spec=pltpu.PrefetchScalarGridSpec(
            num_scalar_prefetch=1, grid=(nb,),
            in_specs=[pl.BlockSpec((blk, ATT_QW), cur)] + band + band,
            out_specs=pl.BlockSpec((blk, ATT_QW), lambda i, s: (i, 0))),
        out_shape=jax.ShapeDtypeStruct((seq, ATT_QW), BF16),
        compiler_params=_cp(("arbitrary",)),
        name="att_window",
    )(sink, qr, k2, k2, k2, k2, v2, v2, v2, v2)
    y_ctx = pl.pallas_call(
        _att_ctx_kernel,
        grid_spec=pltpu.PrefetchScalarGridSpec(
            num_scalar_prefetch=1, grid=(cb,),
            in_specs=[pl.BlockSpec((blk, ATT_QW), lambda i, s: (i, 0)),
                      pl.BlockSpec((ctx_len, kw), cx), pl.BlockSpec((ctx_len, kw), cx)],
            out_specs=pl.BlockSpec((blk, ATT_QW), lambda i, s: (i, 0))),
        out_shape=jax.ShapeDtypeStruct((ctx_len, ATT_QW), BF16),
        compiler_params=_cp(("arbitrary",)),
        name="att_context",
    )(sink, qr, k2, v2)
    return jnp.concatenate([y_ctx, y_lat], axis=0)


def _merge_kernel(a1_ref, a2_ref, a3_ref, w1_ref, w2_ref, w3_ref, g1_ref, g2_ref, g3_ref, o_ref):
    acc = _sigmoid(g1_ref[...].astype(F32)) * _dot(a1_ref[...], w1_ref[...].astype(BF16))
    acc = acc + _sigmoid(g2_ref[...].astype(F32)) * _dot(a2_ref[...], w2_ref[...].astype(BF16))
    acc = acc + _sigmoid(g3_ref[...].astype(F32)) * _dot(a3_ref[...], w3_ref[...].astype(BF16))
    o_ref[...] = acc.astype(BF16)


def merge_branches(y_s5, y_gla, y_att, z, w_s5, w_gla, w_att, l):
    n = z.shape[0]
    tm = _pick(n, (528, 640, 256))
    tn = 512
    a_spec = pl.BlockSpec((tm, S5_WIDTH), lambda j, i: (i, 0))
    w_spec = pl.BlockSpec((None, S5_WIDTH, tn), lambda j, i: (l, 0, j))

    def g_spec(b):
        return pl.BlockSpec((tm, tn), lambda j, i: (i, (OFF_GATE + b * D_MODEL) // tn + j))

    return pl.pallas_call(
        _merge_kernel,
        grid=(D_MODEL // tn, n // tm),
        in_specs=[a_spec, a_spec, a_spec, w_spec, w_spec, w_spec, g_spec(0), g_spec(1), g_spec(2)],
        out_specs=pl.BlockSpec((tm, tn), lambda j, i: (i, j)),
        out_shape=jax.ShapeDtypeStruct((n, D_MODEL), BF16),
        compiler_params=_cp(("arbitrary", "arbitrary")),
        name="merge",
    )(y_s5, y_gla, y_att, w_s5, w_gla, w_att, z, z, z)


def _out_kernel(m_ref, w_ref, x_ref, mod_ref, g_ref, rt_ref, x1_ref, h2_ref, lg_ref, *, ctx_len, tm):
    i = pl.program_id(0)
    is_ctx = (i * tm + lax.broadcasted_iota(jnp.int32, (tm, 1), 0)) < ctx_len
    x1 = x_ref[...] + _row_mods(mod_ref, 2, is_ctx) * _dot(m_ref[...], w_ref[...])
    x1_ref[...] = x1
    r = lax.rsqrt(jnp.mean(x1 * x1, axis=-1, keepdims=True) + EPS)
    h2 = ((x1 * r) * g_ref[0]) * (1.0 + _row_mods(mod_ref, 4, is_ctx)) + _row_mods(mod_ref, 3, is_ctx)
    h_hi = h2.astype(BF16)
    h2_ref[...] = h_hi
    h_lo = (h2 - h_hi.astype(F32)).astype(BF16)
    rt = rt_ref[...]
    r_hi = rt.astype(BF16)
    r_lo = (rt - r_hi.astype(F32)).astype(BF16)
    lg_ref[...] = _dot(h_hi, r_hi) + (_dot(h_lo, r_hi) + _dot(h_hi, r_lo))


def out_projection(m, w_out_bf, xs, mods, norm_g, router_pad, l, ctx_len):
    n = xs.shape[0]
    tm = _pick(n, (528, 640, 256))
    once = pl.Buffered(1)
    return pl.pallas_call(
        functools.partial(_out_kernel, ctx_len=ctx_len, tm=tm),
        grid=(n // tm,),
        in_specs=[pl.BlockSpec((tm, D_MODEL), lambda i: (i, 0)),
                  pl.BlockSpec((None, D_MODEL, D_MODEL), lambda i: (l, 0, 0), pipeline_mode=once),
                  pl.BlockSpec((tm, D_MODEL), lambda i: (i, 0)),
                  pl.BlockSpec((None, 8, N_MOD * D_MODEL), lambda i: (l, 0, 0)),
                  pl.BlockSpec((1, 1, D_MODEL), lambda i: (l, 0, 0)),
                  pl.BlockSpec((None, D_MODEL, LANES), lambda i: (l, 0, 0), pipeline_mode=once)],
        out_specs=[pl.BlockSpec((tm, D_MODEL), lambda i: (i, 0)),
                   pl.BlockSpec((tm, D_MODEL), lambda i: (i, 0)),
                   pl.BlockSpec((tm, LANES), lambda i: (i, 0))],
        out_shape=[jax.ShapeDtypeStruct((n, D_MODEL), F32),
                   jax.ShapeDtypeStruct((n, D_MODEL), BF16),
                   jax.ShapeDtypeStruct((n, LANES), F32)],
        compiler_params=_cp(("arbitrary",)),
        name="out_projection",
    )(m, w_out_bf, xs, mods, norm_g.reshape(DEPTH, 1, D_MODEL), router_pad)


def _route_kernel(lg_ref, aff_ref, slot_ref, slot_t_ref, s0_ref, cnt_ref, *, ctx_len, n):
    tt = TOK_TILE
    lane = lax.broadcasted_iota(jnp.int32, (1, LANES), 1)
    is_expert = lane < N_EXPERTS

    def softmax_tile(b, _):
        r0 = pl.multiple_of(b * tt, tt)
        x = jnp.where(is_expert, lg_ref[pl.ds(r0, tt), :], -jnp.inf)
        e = jnp.exp(x - jnp.max(x, axis=-1, keepdims=True))
        aff_ref[pl.ds(r0, tt), :] = e / jnp.sum(e, axis=-1, keepdims=True)
        return 0

    lax.fori_loop(0, n // tt, softmax_tile, 0)

    ri = lax.broadcasted_iota(jnp.int32, (tt, tt), 0)
    ci = lax.broadcasted_iota(jnp.int32, (tt, tt), 1)
    before = jnp.where(ci < ri, 1.0, 0.0).astype(BF16)

    def bits_of(b):
        r0 = pl.multiple_of(b * tt, tt)
        return pltpu.bitcast(aff_ref[pl.ds(r0, tt), :], jnp.int32)

    def count(b0, b1, pred):
        def body(b, acc):
            return acc + jnp.sum(jnp.where(pred(bits_of(b)), 1.0, 0.0), axis=0, keepdims=True)
        return lax.fori_loop(b0, b1, body, jnp.zeros((1, LANES), F32))

    def route_set(b0, b1, slot_base):
        cap = float(EC_CAPACITY * (b1 - b0) * tt // N_EXPERTS)

        def bisect(_, lh):
            lo, hi = lh
            mid = lo + lax.shift_right_logical(hi - lo + 1, 1)
            ok = count(b0, b1, lambda v: v >= mid) >= cap
            return jnp.where(ok, mid, lo), jnp.where(ok, hi, mid - 1)

        lo0 = jnp.zeros((1, LANES), jnp.int32)
        hi0 = jnp.full((1, LANES), 0x7F800000, jnp.int32)
        thr, _ = lax.fori_loop(0, 31, bisect, (lo0, hi0))
        need = cap - count(b0, b1, lambda v: v > thr)

        def assign(b, carry):
            tie_seen, taken = carry
            r0 = pl.multiple_of(b * tt, tt)
            v = bits_of(b)
            tie = jnp.where(v == thr, 1.0, 0.0)
            tie_rank = _dot(before, tie.astype(BF16)) + tie_seen
            sel = jnp.where((v > thr) | ((v == thr) & (tie_rank < need)), 1.0, 0.0)
            rank = _dot(before, sel.astype(BF16)) + taken
            slot = jnp.where(sel > 0.0, rank + slot_base, -1.0)
            slot_ref[pl.ds(r0, tt), :] = slot.astype(jnp.int32)
            slot_t_ref[:, pl.ds(r0, tt)] = jnp.transpose(slot)[0:N_EXPERTS, :].astype(jnp.int32)
            n_sel = jnp.sum(sel, axis=0, keepdims=True)
            s0_ref[b] = (taken + slot_base).astype(jnp.int32)
            cnt_ref[b] = n_sel.astype(jnp.int32)
            return tie_seen + jnp.sum(tie, axis=0, keepdims=True), taken + n_sel

        zero = jnp.zeros((1, LANES), F32)
        lax.fori_loop(b0, b1, assign, (zero, zero))
        return cap

    cb = ctx_len // tt
    cap_ctx = route_set(0, cb, 0.0)
    route_set(cb, n // tt, cap_ctx)


def route(logits, ctx_len):
    n = logits.shape[0]
    nt = n // TOK_TILE
    full = lambda shape: pl.BlockSpec(shape, lambda i: tuple(0 for _ in shape))
    return pl.pallas_call(
        functools.partial(_route_kernel, ctx_len=ctx_len, n=n),
        grid=(1,),
        in_specs=[full((n, LANES))],
        out_specs=[full((n, LANES)), full((n, LANES)), full((N_EXPERTS, n)), full((nt, 1, LANES)),
                   full((nt, 1, LANES))],
        out_shape=[jax.ShapeDtypeStruct((n, LANES), F32),
                   jax.ShapeDtypeStruct((n, LANES), jnp.int32),
                   jax.ShapeDtypeStruct((N_EXPERTS, n), jnp.int32),
                   jax.ShapeDtypeStruct((nt, 1, LANES), jnp.int32),
                   jax.ShapeDtypeStruct((nt, 1, LANES), jnp.int32)],
        compiler_params=_cp(("arbitrary",)),
        name="route",
    )(logits)


def _window_plan(s0, cnt, rows, align_bits=3):
    a0 = lax.shift_left(lax.shift_right_logical(s0, align_bits), align_bits)
    nwin = jnp.where(cnt > 0, lax.shift_right_logical(s0 - a0 + cnt + MOE_WIN - 1, MOE_WIN.bit_length() - 1), 0)
    return a0, nwin, rows - MOE_WIN


GATHER_GROUP = 4
BF16_ROWS_LOG2 = 4


def _gather_kernel(s0_ref, cnt_ref, h_ref, slot_ref, xs_ref, *, rows):
    g = pl.program_id(0)
    t = pl.program_id(1)

    @pl.when(t == 0)
    def _():
        xs_ref[...] = jnp.zeros_like(xs_ref)

    def window(i, j, a0, last):
        start = a0 + j * MOE_WIN
        base = pl.multiple_of(jnp.minimum(start, last), 1 << BF16_ROWS_LOG2)
        want = base + lax.broadcasted_iota(jnp.int32, (MOE_WIN, 1), 0)
        want = jnp.where(want >= start, want, -2)
        return base, jnp.where(slot_ref[i] == want, 1.0, 0.0).astype(BF16)

    def add_rows(i, base, new):
        prev = xs_ref[i, pl.ds(base, MOE_WIN), :].astype(F32)
        xs_ref[i, pl.ds(base, MOE_WIN), :] = (prev + new).astype(BF16)

    plans = [_window_plan(s0_ref[t, g * GATHER_GROUP + i], cnt_ref[t, g * GATHER_GROUP + i], rows, BF16_ROWS_LOG2)
             for i in range(GATHER_GROUP)]
    firsts = [window(i, 0, plans[i][0], plans[i][2]) for i in range(GATHER_GROUP)]
    packed = _dot(jnp.concatenate([oh for _, oh in firsts], axis=0), h_ref[...])
    for i in range(GATHER_GROUP):
        add_rows(i, firsts[i][0], packed[i * MOE_WIN:(i + 1) * MOE_WIN])

        def body(j, _, i=i):
            base, onehot = window(i, j, plans[i][0], plans[i][2])
            add_rows(i, base, _dot(onehot, h_ref[...]))
            return 0

        lax.fori_loop(1, plans[i][1], body, 0)


def moe_gather(h2, slot_t, s0, cnt, rows):
    n = h2.shape[0]
    nt = n // TOK_TILE
    gg = GATHER_GROUP
    return pl.pallas_call(
        functools.partial(_gather_kernel, rows=rows),
        grid_spec=pltpu.PrefetchScalarGridSpec(
            num_scalar_prefetch=2, grid=(N_EXPERTS // gg, nt),
            in_specs=[pl.BlockSpec((TOK_TILE, D_MODEL), lambda g, t, a, b: (t, 0)),
                      pl.BlockSpec((gg, 1, TOK_TILE), lambda g, t, a, b: (g, 0, t))],
            out_specs=pl.BlockSpec((gg, rows, D_MODEL), lambda g, t, a, b: (g, 0, 0))),
        out_shape=jax.ShapeDtypeStruct((N_EXPERTS, rows, D_MODEL), BF16),
        compiler_params=_cp(("arbitrary", "arbitrary")),
        name="moe_gather",
    )(s0, cnt, h2, slot_t.reshape(N_EXPERTS, 1, n))


def _ffn_kernel(xs_ref, wg_ref, wu_ref, wd_ref, y_ref, act_scr):
    f = pl.program_id(1)
    nf = EXPERT_FF // FF_TILE

    @pl.when(f < nf)
    def _():
        x = xs_ref[0]
        a = _dot(x, wg_ref[0].astype(BF16))
        u = _dot(x, wu_ref[0].astype(BF16))
        act_scr[:, pl.ds(pl.multiple_of(f * FF_TILE, FF_TILE), FF_TILE)] = ((a * _sigmoid(a)) * u).astype(BF16)

    @pl.when(f >= nf)
    def _():
        y_ref[0] = _dot(act_scr[...], wd_ref[0].astype(BF16)).astype(BF16)


def moe_ffn(xs, w_gate, w_up, w_down, l):
    rows = xs.shape[1]
    nf = EXPERT_FF // FF_TILE
    up_idx = lambda e, f: (l, e, 0, jnp.minimum(f, nf - 1))
    return pl.pallas_call(
        _ffn_kernel,
        grid=(N_EXPERTS, nf + D_MODEL // FF_TILE),
        in_specs=[pl.BlockSpec((1, rows, D_MODEL), lambda e, f: (e, 0, 0)),
                  pl.BlockSpec((None, 1, D_MODEL, FF_TILE), up_idx),
                  pl.BlockSpec((None, 1, D_MODEL, FF_TILE), up_idx),
                  pl.BlockSpec((None, 1, EXPERT_FF, FF_TILE), lambda e, f: (l, e, 0, jnp.maximum(f - nf, 0)))],
        out_specs=pl.BlockSpec((1, rows, FF_TILE), lambda e, f: (e, 0, jnp.maximum(f - nf, 0))),
        out_shape=jax.ShapeDtypeStruct((N_EXPERTS, rows, D_MODEL), BF16),
        scratch_shapes=[pltpu.VMEM((rows, EXPERT_FF), BF16)],
        compiler_params=_cp(("arbitrary", "arbitrary")),
        name="moe_ffn",
    )(xs, w_gate, w_up, w_down)


MORE_WINS = TOK_TILE // MOE_WIN


def _combine_kernel(s0_ref, cnt_ref, x_ref, slot_ref, aff_ref, mod_ref, g_ref, y_hbm, o_ref, buf0, bufx, sem0, semx,
                    *, rows, ctx_len, final):
    t = pl.program_id(0)
    nt = pl.num_programs(0)
    tt = TOK_TILE

    def first_copy(tile, e, par):
        a0, _, last = _window_plan(s0_ref[tile, e], cnt_ref[tile, e], rows)
        base = pl.multiple_of(jnp.minimum(a0, last), 8)
        cp = pltpu.make_async_copy(y_hbm.at[e, pl.ds(base, MOE_WIN)],
                                   buf0.at[par, pl.ds(e * MOE_WIN, MOE_WIN)], sem0.at[par, e])
        return base, a0, cp

    def more_copy(e, j, a0, last):
        base = pl.multiple_of(jnp.minimum(a0 + j * MOE_WIN, last), 8)
        k = e * MORE_WINS + j - 1
        return base, pltpu.make_async_copy(y_hbm.at[e, pl.ds(base, MOE_WIN)], bufx.at[k], semx.at[k])

    @pl.when(t == 0)
    def _():
        for e in range(N_EXPERTS):
            first_copy(0, e, 0)[2].start()

    @pl.when(t + 1 < nt)
    def _():
        for e in range(N_EXPERTS):
            first_copy(t + 1, e, (t + 1) % 2)[2].start()

    for e in range(N_EXPERTS):
        a0, nwin, last = _window_plan(s0_ref[t, e], cnt_ref[t, e], rows)

        def start(j, _, e=e, a0=a0, last=last):
            more_copy(e, j, a0, last)[1].start()
            return 0

        lax.fori_loop(1, nwin, start, 0)

    par = t % 2
    lane = lax.broadcasted_iota(jnp.int32, (1, LANES), 1)
    left = lane < MOE_WIN
    offs = jnp.where(left, lane, lane - MOE_WIN)
    cols_hi, cols_lo = [], []
    for c in range(N_EXPERTS * MOE_WIN // LANES):
        e0, e1 = 2 * c, 2 * c + 1
        b0, a00, cp0 = first_copy(t, e0, par)
        b1, a01, cp1 = first_copy(t, e1, par)
        cp0.wait()
        cp1.wait()
        want = jnp.where(left, b0, b1) + offs
        want = jnp.where(want >= jnp.where(left, a00, a01), want, -2)
        picked = jnp.where(left, slot_ref[:, e0:e0 + 1], slot_ref[:, e1:e1 + 1]) == want
        gate = jnp.where(picked, jnp.where(left, aff_ref[:, e0:e0 + 1], aff_ref[:, e1:e1 + 1]), 0.0)
        g_hi = gate.astype(BF16)
        cols_hi.append(g_hi)
        cols_lo.append((gate - g_hi.astype(F32)).astype(BF16))
    y0 = buf0[par]
    o_ref[...] = _dot(jnp.concatenate(cols_hi, axis=-1), y0) + _dot(jnp.concatenate(cols_lo, axis=-1), y0)

    for e in range(N_EXPERTS):
        a0, nwin, last = _window_plan(s0_ref[t, e], cnt_ref[t, e], rows)

        def take(j, _, e=e, a0=a0, last=last):
            base, cp = more_copy(e, j, a0, last)
            cp.wait()
            want = base + lax.broadcasted_iota(jnp.int32, (1, MOE_WIN), 1)
            want = jnp.where(want >= a0 + j * MOE_WIN, want, -2)
            onehot = jnp.where(slot_ref[:, e:e + 1] == want, 1.0, 0.0).astype(BF16)
            o_ref[...] += aff_ref[:, e:e + 1] * _dot(onehot, bufx[e * MORE_WINS + j - 1])
            return 0

        lax.fori_loop(1, nwin, take, 0)

    is_ctx = (t * tt + lax.broadcasted_iota(jnp.int32, (tt, 1), 0)) < ctx_len
    x2 = x_ref[...] + _row_mods(mod_ref, 5, is_ctx) * o_ref[...]
    if final:
        r = lax.rsqrt(jnp.mean(x2 * x2, axis=-1, keepdims=True) + EPS)
        x2 = (x2 * r) * g_ref[...]
    o_ref[...] = x2


def moe_combine(x1, slot, aff, s0, cnt, mods, y, final_g, l, ctx_len, final):
    n = x1.shape[0]
    rows = y.shape[1]
    tt = TOK_TILE
    nbuf = N_EXPERTS * MORE_WINS
    return pl.pallas_call(
        functools.partial(_combine_kernel, rows=rows, ctx_len=ctx_len, final=final),
        grid_spec=pltpu.PrefetchScalarGridSpec(
            num_scalar_prefetch=2, grid=(n // tt,),
            in_specs=[pl.BlockSpec((tt, D_MODEL), lambda t, a, b: (t, 0)),
                      pl.BlockSpec((tt, LANES), lambda t, a, b: (t, 0)),
                      pl.BlockSpec((tt, LANES), lambda t, a, b: (t, 0)),
                      pl.BlockSpec((None, 8, N_MOD * D_MODEL), lambda t, a, b: (l, 0, 0)),
                      pl.BlockSpec((1, D_MODEL), lambda t, a, b: (0, 0)),
                      pl.BlockSpec(memory_space=pl.ANY)],
            out_specs=pl.BlockSpec((tt, D_MODEL), lambda t, a, b: (t, 0)),
            scratch_shapes=[pltpu.VMEM((2, N_EXPERTS * MOE_WIN, D_MODEL), BF16),
                            pltpu.VMEM((nbuf, MOE_WIN, D_MODEL), BF16),
                            pltpu.SemaphoreType.DMA((2, N_EXPERTS)),
                            pltpu.SemaphoreType.DMA((nbuf,))]),
        out_shape=jax.ShapeDtypeStruct((n, D_MODEL), F32),
        compiler_params=_cp(("arbitrary",)),
        name="moe_combine",
    )(s0, cnt, x1, slot, aff, mods, final_g.reshape(1, D_MODEL), y)


def rope_tables(seq, ctx_len):
    pos = jnp.arange(seq)
    row = (pos // GRID_W).astype(F32)
    col = (pos % GRID_W).astype(F32)
    n_freq = ATT_HEAD_DIM // 4
    inv_freq = ROPE_BASE ** (-jnp.arange(n_freq, dtype=F32) / n_freq)
    ang = jnp.concatenate([row[:, None] * inv_freq, col[:, None] * inv_freq], axis=-1)
    cos, sin = jnp.cos(ang), jnp.sin(ang)
    cos_t = jnp.concatenate([jnp.ones((ctx_len, LANES), F32), jnp.tile(cos, (1, 4))], axis=0)
    sin_t = jnp.concatenate([jnp.zeros((ctx_len, LANES), F32), jnp.tile(jnp.concatenate([-sin, sin], -1), (1, 2))],
                            axis=0)
    return cos_t, sin_t


def kernel(x, c, ctx, c_ctx, ada_w, ada_b, norm1_g, norm2_g, w_in, s5_lam_re, s5_lam_im, s5_log_dt,
           s5_b_re, s5_b_im, s5_c_re, s5_c_im, s5_d, s5_w_glu, gla_w1, gla_w2, gla_b, gla_norm_g,
           attn_sink, w_branch_s5, w_branch_gla, w_branch_attn, w_out, moe_router, moe_w_gate,
           moe_w_up, moe_w_down, final_g):
    seq, ctx_len = x.shape[1], ctx.shape[1]
    n = seq + ctx_len
    assert x.shape[0] == 1 and ctx_len % TOK_TILE == 0 and seq % TOK_TILE == 0
    xs = jnp.concatenate([ctx[0], x[0]], axis=0)
    c8 = jnp.zeros((8, D_MODEL), F32).at[0].set(c[0]).at[1].set(c_ctx)
    mods = ada_mods(c8, ada_w, ada_b)
    cos_t, sin_t = rope_tables(seq, ctx_len)
    w_out_bf = w_out.astype(BF16)
    w_in_bf = w_in.astype(BF16)
    router_pad = jnp.pad(moe_router, ((0, 0), (0, 0), (0, LANES - N_EXPERTS)))
    rows = EC_CAPACITY * n // N_EXPERTS
    for l in range(DEPTH):
        w1cat = jnp.pad(jnp.concatenate([gla_w1[l, 0], gla_w1[l, 1]], axis=-1),
                        ((0, 0), (0, LANES - 2 * GLA_RANK)))
        w2pad = jnp.zeros((2, LANES, GLA_KW), F32)
        w2pad = w2pad.at[0, 0:GLA_RANK].set(gla_w2[l, 0]).at[1, GLA_RANK:2 * GLA_RANK].set(gla_w2[l, 1])
        zu, z, lr = in_projection(xs, norm1_g, mods, w_in_bf, w1cat, l, ctx_len)
        s5p = s5_params(s5_lam_re[l], s5_lam_im[l], s5_log_dt[l], s5_b_re[l], s5_b_im[l])
        y_s5 = s5_glu(s5_mix(zu, s5p, s5_c_re[l], s5_c_im[l], s5_d[l], ctx_len), s5_w_glu, l)
        gla_bias = gla_b[l].reshape(2, 1, GLA_KW)
        o_f, o_b = gla_mix(z, lr, w2pad, gla_bias, ctx_len)
        y_gla = gla_post(o_f, o_b, z, gla_norm_g, l)
        qr, k2, v2 = att_prep(z, cos_t, sin_t)
        y_att = attention(qr, k2, v2, attn_sink[l], ctx_len)
        m = merge_branches(y_s5, y_gla, y_att, z, w_branch_s5, w_branch_gla, w_branch_attn, l)
        x1, h2, logits = out_projection(m, w_out_bf, xs, mods, norm2_g, router_pad, l, ctx_len)
        aff, slot, slot_t, s0, cnt = route(logits, ctx_len)
        s0, cnt = s0[:, 0, :N_EXPERTS], cnt[:, 0, :N_EXPERTS]
        xe = moe_gather(h2, slot_t, s0, cnt, rows)
        ye = moe_ffn(xe, moe_w_gate, moe_w_up, moe_w_down, l)
        xs = moe_combine(x1, slot, aff, s0, cnt, mods, ye, final_g, l, ctx_len, l == DEPTH - 1)
    return xs[ctx_len:][None]
```

```python
import functools
import math

import jax
import jax.numpy as jnp
from jax import lax
from jax.experimental import pallas as pl
from jax.experimental.pallas import tpu as pltpu

D_MODEL = 2048
DEPTH = 2
GRID_W = 64
EPS = 1e-6
N_MOD = 6
S5_WIDTH = 1024
S5_GROUP_CH = 16
S5_GROUPS = S5_WIDTH // S5_GROUP_CH
S5_STATE = 64
GLA_HEADS = 4
GLA_DK = 128
GLA_DV = 256
GLA_KW = GLA_HEADS * GLA_DK
GLA_VW = GLA_HEADS * GLA_DV
GLA_RANK = 16
GLA_TAU = 16.0
GLA_CHUNK = 64
ATT_HEADS = 16
ATT_KV_HEADS = 4
ATT_HEAD_DIM = 64
ATT_QW = ATT_HEADS * ATT_HEAD_DIM
ATT_KVW = ATT_KV_HEADS * ATT_HEAD_DIM
WINDOW = 128
ATT_BLOCK = 128
ROPE_BASE = 10000.0
N_BRANCH = 3
IN_SPLITS = (S5_WIDTH, GLA_KW, GLA_KW, GLA_VW, GLA_VW, ATT_QW, ATT_KVW, ATT_KVW, N_BRANCH * D_MODEL)
IN_WIDTH = sum(IN_SPLITS)
N_EXPERTS = 16
EXPERT_FF = 2048
EC_CAPACITY = 2

OFF_GQ = 0
OFF_GK = OFF_GQ + GLA_KW
OFF_GV = OFF_GK + GLA_KW
OFF_GR = OFF_GV + GLA_VW
OFF_AQ = OFF_GR + GLA_VW
OFF_AK = OFF_AQ + ATT_QW
OFF_AV = OFF_AK + ATT_KVW
OFF_GATE = OFF_AV + ATT_KVW

LANES = 128
S5_T = 16
S5_PAIRS = S5_GROUPS // 2
S5_BLOCKS = S5_WIDTH // LANES
S5_BLK_GROUPS = LANES // S5_GROUP_CH
S5_BLK_PAIRS = S5_BLK_GROUPS // 2
S5_BLK_STATE = S5_BLK_GROUPS * S5_STATE
TOK_TILE = 256
MOE_WIN = 64
FF_TILE = 512
VMEM_MB = 56

F32 = jnp.float32
BF16 = jnp.bfloat16
HI = lax.Precision.HIGHEST


def _cp(sem, mb=VMEM_MB):
    return pltpu.CompilerParams(dimension_semantics=sem, vmem_limit_bytes=mb << 20)


def _dot(a, b):
    return jnp.dot(a, b, preferred_element_type=F32)


def _dot_nt(a, b):
    return lax.dot_general(a, b, (((1,), (1,)), ((), ())), preferred_element_type=F32)


def _dot_tn(a, b):
    return lax.dot_general(a, b, (((0,), (0,)), ((), ())), preferred_element_type=F32)


def _pick(n, cands):
    for c in cands:
        if n % c == 0:
            return c
    raise ValueError(f"no tile for {n} in {cands}")


def _sigmoid(x):
    return 1.0 / (1.0 + jnp.exp(-x))


def _row_mods(mod_ref, k, is_ctx):
    lo, hi = k * D_MODEL, (k + 1) * D_MODEL
    return jnp.where(is_ctx, mod_ref[1:2, lo:hi], mod_ref[0:1, lo:hi])


def _ada_kernel(c_ref, w_ref, b_ref, o_ref):
    cc = c_ref[...]
    o_ref[0] = jnp.dot(cc * _sigmoid(cc), w_ref[0], preferred_element_type=F32, precision=HI) + b_ref[0]


def ada_mods(c8, ada_w, ada_b):
    tn = 1024
    width = N_MOD * D_MODEL
    return pl.pallas_call(
        _ada_kernel,
        grid=(DEPTH, width // tn),
        in_specs=[pl.BlockSpec((8, D_MODEL), lambda l, j: (0, 0)),
                  pl.BlockSpec((1, D_MODEL, tn), lambda l, j: (l, 0, j)),
                  pl.BlockSpec((1, 1, tn), lambda l, j: (l, 0, j))],
        out_specs=pl.BlockSpec((1, 8, tn), lambda l, j: (l, 0, j)),
        out_shape=jax.ShapeDtypeStruct((DEPTH, 8, width), F32),
        compiler_params=_cp(("arbitrary", "arbitrary")),
        name="ada_mods",
    )(c8, ada_w, ada_b.reshape(DEPTH, 1, width))


def _in_kernel(x_ref, g_ref, mod_ref, w_ref, w1_ref, zu_ref, zr_ref, lr_ref, h_scr, *, ctx_len, tm, nu):
    i = pl.program_id(0)
    j = pl.program_id(1)

    @pl.when(j == 0)
    def _():
        x = x_ref[...]
        r = lax.rsqrt(jnp.mean(x * x, axis=-1, keepdims=True) + EPS)
        hn = (x * r) * g_ref[0]
        is_ctx = (i * tm + lax.broadcasted_iota(jnp.int32, (tm, 1), 0)) < ctx_len
        hb = (hn * (1.0 + _row_mods(mod_ref, 1, is_ctx)) + _row_mods(mod_ref, 0, is_ctx)).astype(BF16)
        h_scr[...] = hb
        lr_ref[...] = _dot(hb, w1_ref[...].astype(BF16))

    z = _dot(h_scr[...], w_ref[...].astype(BF16))

    @pl.when(j < nu)
    def _():
        zu_ref[...] = z

    @pl.when(j >= nu)
    def _():
        zr_ref[...] = z.astype(BF16)


def in_projection(xs, norm_g, mods, w_in, w1cat, l, ctx_len):
    n = xs.shape[0]
    tm = _pick(n, (1056, 640, 256))
    tn = 512
    nu = S5_WIDTH // tn
    return pl.pallas_call(
        functools.partial(_in_kernel, ctx_len=ctx_len, tm=tm, nu=nu),
        grid=(n // tm, IN_WIDTH // tn),
        in_specs=[pl.BlockSpec((tm, D_MODEL), lambda i, j: (i, 0)),
                  pl.BlockSpec((1, 1, D_MODEL), lambda i, j: (l, 0, 0)),
                  pl.BlockSpec((None, 8, N_MOD * D_MODEL), lambda i, j: (l, 0, 0)),
                  pl.BlockSpec((None, D_MODEL, tn), lambda i, j: (l, 0, j)),
                  pl.BlockSpec((D_MODEL, LANES), lambda i, j: (0, 0))],
        out_specs=[pl.BlockSpec((tm, tn), lambda i, j: (i, jnp.minimum(j, nu - 1))),
                   pl.BlockSpec((tm, tn), lambda i, j: (i, jnp.maximum(j - nu, 0))),
                   pl.BlockSpec((tm, LANES), lambda i, j: (i, 0))],
        out_shape=[jax.ShapeDtypeStruct((n, S5_WIDTH), F32),
                   jax.ShapeDtypeStruct((n, IN_WIDTH - S5_WIDTH), BF16),
                   jax.ShapeDtypeStruct((n, LANES), F32)],
        scratch_shapes=[pltpu.VMEM((tm, D_MODEL), BF16)],
        compiler_params=_cp(("arbitrary", "arbitrary")),
        name="in_projection",
    )(xs, norm_g.reshape(DEPTH, 1, D_MODEL), mods, w_in, w1cat)


def s5_params(lam_re, lam_im, log_dt, b_re, b_im):
    dt = jnp.exp(log_dt)[..., None]
    mag = jnp.exp(lam_re * dt)
    ab_re = mag * jnp.cos(lam_im * dt)
    ab_im = mag * jnp.sin(lam_im * dt)
    den = lam_re * lam_re + lam_im * lam_im
    nr = ab_re - 1
    f_re = ((nr * lam_re + ab_im * lam_im) / den)[:, :, None, :]
    f_im = ((ab_im * lam_re - nr * lam_im) / den)[:, :, None, :]
    bt_re, bt_im = b_re.transpose(0, 1, 3, 2), b_im.transpose(0, 1, 3, 2)
    bb_re = f_re * bt_re - f_im * bt_im
    bb_im = f_re * bt_im + f_im * bt_re
    tau = jnp.arange(S5_T + 1, dtype=F32)[None, :, None, None]
    pmag = jnp.exp((lam_re * dt)[:, None] * tau)
    ang = (lam_im * dt)[:, None] * tau
    return pmag * jnp.cos(ang), pmag * jnp.sin(ang), bb_re, bb_im


def _s5_chunk_rows(z_ref, nch):
    return jnp.concatenate([z_ref[pl.ds(t, nch, stride=S5_T), :].astype(BF16) for t in range(S5_T)], axis=-1)


def _s5_place(stage, pr, pi, wre_ref, wim_ref, d, im_sign):
    ch, n_st = S5_GROUP_CH, S5_STATE
    for g in range(S5_BLK_GROUPS):
        a_r, a_i = pr[g:g + 1, :], pi[g:g + 1, :]
        w_r, w_i = wre_ref[d, g], wim_ref[d, g]
        stage[g * ch:(g + 1) * ch, g * n_st:(g + 1) * n_st] = a_r * w_r - a_i * w_i
        stage[g * ch:(g + 1) * ch, S5_BLK_STATE + g * n_st:S5_BLK_STATE + (g + 1) * n_st] = (
            im_sign * (a_r * w_i + a_i * w_r))


def _s5_state_in_kernel(z_ref, pre_ref, pim_ref, bre_ref, bim_ref, vre_ref, vim_ref, wq_scr, stage, *, nch):
    @pl.when((pl.program_id(0) == 0) & (pl.program_id(1) == 0))
    def _():
        stage[...] = jnp.zeros_like(stage)

    rev = pl.program_id(1) == 1
    for t in range(S5_T):
        e_t = jnp.where(rev, t, S5_T - 1 - t)
        _s5_place(stage, pre_ref[0, e_t], pim_ref[0, e_t], bre_ref, bim_ref, 0, 1.0)
        wq_scr[t * LANES:(t + 1) * LANES, :] = stage[...].astype(BF16)
    v = _dot(_s5_chunk_rows(z_ref, nch), wq_scr[...])
    vre_ref[0] = v[:, :S5_BLK_STATE]
    vim_ref[0] = v[:, S5_BLK_STATE:]


def _s5_scan_kernel(vre_ref, vim_ref, are_ref, aim_ref, sre_ref, sim_ref, *, nch, ncc):
    rev = pl.program_id(0) == 1
    ar = are_ref[0]
    ai = aim_ref[0]

    def body(i, carry):
        sr, si = carry
        k = jnp.where(rev, jnp.where(i < ncc, ncc - 1 - i, nch + ncc - 1 - i), i)
        sre_ref[0, k] = sr
        sim_ref[0, k] = si
        return ar * sr - ai * si + vre_ref[0, k], ar * si + ai * sr + vim_ref[0, k]

    zero = jnp.zeros((S5_BLOCKS, LANES), F32)
    lax.fori_loop(0, nch, body, (zero, zero))


def _s5_out_kernel(z_ref, sre_ref, sim_ref, pre_ref, pim_ref, bre_ref, bim_ref, cre_ref, cim_ref, d_ref, y_ref,
                   wq_scr, wt_scr, stage, bb_scr, *, nch):
    @pl.when(pl.program_id(0) == 0)
    def _():
        stage[...] = jnp.zeros_like(stage)
        bb_scr[...] = jnp.zeros_like(bb_scr)

    one = jnp.ones((1, S5_STATE), F32)
    zero = jnp.zeros((1, S5_STATE), F32)
    ones_re = jnp.concatenate([one] * S5_BLK_GROUPS, axis=0)
    zeros_im = jnp.concatenate([zero] * S5_BLK_GROUPS, axis=0)
    for d in range(2):
        _s5_place(bb_scr.at[d], ones_re, zeros_im, bre_ref, bim_ref, d, 1.0)
        bb = bb_scr[d]
        bb_hi = bb.astype(BF16)
        bb_lo = (bb - bb_hi.astype(F32)).astype(BF16)
        for tau in range(S5_T + 1):
            _s5_place(stage, pre_ref[d, tau], pim_ref[d, tau], cre_ref, cim_ref, d, -1.0)
            blk = stage[...]
            blk_hi = blk.astype(BF16)
            if tau < S5_T:
                k = S5_T - 1 - tau if d == 0 else tau
                blk_lo = (blk - blk_hi.astype(F32)).astype(BF16)
                kq = _dot_nt(bb_hi, blk_hi) + (_dot_nt(bb_lo, blk_hi) + _dot_nt(bb_hi, blk_lo))
                wt_scr[d, k * LANES:(k + 1) * LANES, :] = kq.astype(BF16)
            if tau > 0:
                t = tau - 1 if d == 0 else S5_T - tau
                wq_scr[d, t * LANES:(t + 1) * LANES, :] = blk_hi
    rows = _s5_chunk_rows(z_ref, nch)
    carried = None
    for d in range(2):
        sp = jnp.concatenate([sre_ref[d], sim_ref[d]], axis=-1).astype(BF16)
        part = _dot_nt(sp, wq_scr[d])
        carried = part if carried is None else carried + part
    for t in range(S5_T):
        acc = z_ref[pl.ds(t, nch, stride=S5_T), :] * d_ref[...] + carried[:, t * LANES:(t + 1) * LANES]
        acc = acc + _dot(rows[:, :LANES * (t + 1)], wt_scr[0, LANES * (S5_T - 1 - t):, :])
        acc = acc + _dot(rows[:, LANES * t:], wt_scr[1, :LANES * (S5_T - t), :])
        y_ref[pl.ds(t, nch, stride=S5_T), :] = acc


def s5_mix(z, params, c_re, c_im, d_skip, ctx_len):
    n = z.shape[0]
    nch, ncc = n // S5_T, ctx_len // S5_T
    p_re, p_im, bb_re, bb_im = params
    st, tl = S5_BLK_STATE, S5_T * LANES
    gb, ch, n_st = S5_BLK_GROUPS, S5_GROUP_CH, S5_STATE
    a_re = p_re[:, S5_T].reshape(2, S5_BLOCKS, st)
    a_im = p_im[:, S5_T].reshape(2, S5_BLOCKS, st)
    v_shape = jax.ShapeDtypeStruct((2, nch, S5_BLOCKS * st), F32)
    pow_spec = pl.BlockSpec((1, S5_T + 1, gb, n_st), lambda q, d: (d, 0, q, 0))
    par_spec = pl.BlockSpec((1, gb, ch, n_st), lambda q, d: (d, q, 0, 0))
    v_re, v_im = pl.pallas_call(
        functools.partial(_s5_state_in_kernel, nch=nch),
        grid=(S5_BLOCKS, 2),
        in_specs=[pl.BlockSpec((n, LANES), lambda q, d: (0, q)), pow_spec, pow_spec, par_spec, par_spec],
        out_specs=[pl.BlockSpec((1, nch, st), lambda q, d: (d, 0, q))] * 2,
        out_shape=[v_shape, v_shape],
        scratch_shapes=[pltpu.VMEM((tl, 2 * st), BF16), pltpu.VMEM((LANES, 2 * st), F32)],
        compiler_params=_cp(("arbitrary", "arbitrary")),
        name="s5_state_in",
    )(z, p_re, p_im, bb_re, bb_im)
    blocked = (2, nch, S5_BLOCKS, st)
    v_spec = pl.BlockSpec((1, nch, S5_BLOCKS, LANES), lambda d, j: (d, 0, 0, j))
    a_spec = pl.BlockSpec((1, S5_BLOCKS, LANES), lambda d, j: (d, 0, j))
    s_shape = jax.ShapeDtypeStruct(blocked, F32)
    s_re, s_im = pl.pallas_call(
        functools.partial(_s5_scan_kernel, nch=nch, ncc=ncc),
        grid=(2, st // LANES),
        in_specs=[v_spec, v_spec, a_spec, a_spec],
        out_specs=[v_spec, v_spec],
        out_shape=[s_shape, s_shape],
        compiler_params=_cp(("arbitrary", "arbitrary")),
        name="s5_scan",
    )(v_re.reshape(blocked), v_im.reshape(blocked), a_re, a_im)
    s_spec = pl.BlockSpec((2, nch, st), lambda q: (0, 0, q))
    pow2_spec = pl.BlockSpec((2, S5_T + 1, gb, n_st), lambda q: (0, 0, q, 0))
    par2_spec = pl.BlockSpec((2, gb, ch, n_st), lambda q: (0, q, 0, 0))
    return pl.pallas_call(
        functools.partial(_s5_out_kernel, nch=nch),
        grid=(S5_BLOCKS,),
        in_specs=[pl.BlockSpec((n, LANES), lambda q: (0, q)), s_spec, s_spec, pow2_spec, pow2_spec,
                  par2_spec, par2_spec, par2_spec, par2_spec, pl.BlockSpec((1, LANES), lambda q: (0, q))],
        out_specs=pl.BlockSpec((n, LANES), lambda q: (0, q)),
        out_shape=jax.ShapeDtypeStruct((n, S5_WIDTH), F32),
        scratch_shapes=[pltpu.VMEM((2, tl, 2 * st), BF16), pltpu.VMEM((2, tl, LANES), BF16),
                        pltpu.VMEM((LANES, 2 * st), F32), pltpu.VMEM((2, LANES, 2 * st), F32)],
        compiler_params=_cp(("arbitrary",)),
        name="s5_out",
    )(z, s_re.reshape(2, nch, S5_BLOCKS * st), s_im.reshape(2, nch, S5_BLOCKS * st), p_re, p_im,
      bb_re, bb_im, c_re, c_im, d_skip.reshape(1, S5_WIDTH))


def _s5_glu_kernel(y_ref, w_ref, o_ref):
    y = y_ref[...]
    g = 0.5 * y * (1.0 + jnp.tanh(math.sqrt(2.0 / math.pi) * (y + 0.044715 * (y * y * y))))
    o_ref[...] = (g * _sigmoid(_dot(g.astype(BF16), w_ref[...].astype(BF16)))).astype(BF16)


def s5_glu(y, w_glu, l):
    n = y.shape[0]
    tm = _pick(n, (1056, 640, 256))
    return pl.pallas_call(
        _s5_glu_kernel,
        grid=(n // tm,),
        in_specs=[pl.BlockSpec((tm, S5_WIDTH), lambda i: (i, 0)),
                  pl.BlockSpec((None, S5_WIDTH, S5_WIDTH), lambda i: (l, 0, 0))],
        out_specs=pl.BlockSpec((tm, S5_WIDTH), lambda i: (i, 0)),
        out_shape=jax.ShapeDtypeStruct((n, S5_WIDTH), BF16),
        compiler_params=_cp(("arbitrary",)),
        name="s5_glu",
    )(y, w_glu)


def _gla_kernel(qf_ref, kf_ref, vf_ref, lrf_ref, qb_ref, kb_ref, vb_ref, lrb_ref, w2_ref, b_ref, of_ref, ob_ref,
                st_scr):
    @pl.when(pl.program_id(0) == 0)
    def _():
        st_scr[...] = jnp.zeros_like(st_scr)

    _gla_chunk(qf_ref, kf_ref, vf_ref, lrf_ref, w2_ref.at[0], b_ref.at[0], of_ref, st_scr.at[0], False)
    _gla_chunk(qb_ref, kb_ref, vb_ref, lrb_ref, w2_ref.at[1], b_ref.at[1], ob_ref, st_scr.at[1], True)


def _gla_chunk(q_ref, k_ref, v_ref, lr_ref, w2_ref, b_ref, o_ref, st_scr, rev):
    c = GLA_CHUNK
    x = _dot(lr_ref[...].astype(BF16), w2_ref[...].astype(BF16)) + b_ref[...]
    la = (jnp.minimum(x, 0.0) - jnp.log1p(jnp.exp(-jnp.abs(x)))) * (1.0 / GLA_TAU)
    ri = lax.broadcasted_iota(jnp.int32, (c, c), 0)
    ci = lax.broadcasted_iota(jnp.int32, (c, c), 1)
    keep = (ci >= ri) if rev else (ci <= ri)
    tri = jnp.where(keep, 1.0, 0.0).astype(BF16)
    hi = la.astype(BF16)
    r1 = la - hi.astype(F32)
    mid = r1.astype(BF16)
    lo = (r1 - mid.astype(F32)).astype(BF16)
    bc = _dot(tri, hi) + _dot(tri, mid) + _dot(tri, lo)
    last, ref_row = (0, c // 2) if rev else (c - 1, c // 2 - 1)
    btot = bc[last:last + 1]
    bmid = bc[ref_row:ref_row + 1]
    qs = q_ref[...].astype(F32) * (GLA_DK ** -0.5)
    kk = k_ref[...].astype(F32)
    q_mid = (qs * jnp.exp(bc - bmid)).astype(BF16)
    k_mid = (kk * jnp.exp(bmid - bc)).astype(BF16)
    q_in = (qs * jnp.exp(bc)).astype(BF16)
    k_out = (kk * jnp.exp(btot - bc)).astype(BF16)
    decay = jnp.exp(btot)
    for h in range(GLA_HEADS):
        ks = slice(h * GLA_DK, (h + 1) * GLA_DK)
        vs = slice(h * GLA_DV, (h + 1) * GLA_DV)
        vb = v_ref[:, vs].astype(BF16)
        sc = jnp.where(keep, _dot_nt(q_mid[:, ks], k_mid[:, ks]), 0.0).astype(BF16)
        st = st_scr[h]
        o_ref[:, vs] = _dot(sc, vb) + _dot_nt(q_in[:, ks], st.astype(BF16))
        st_scr[h] = decay[:, ks] * st + _dot_tn(vb, k_out[:, ks])


def gla_mix(z, lr, w2pad, bias, ctx_len):
    n = z.shape[0]
    c = GLA_CHUNK
    nch, ncc = n // c, ctx_len // c
    fwd = lambda i: i
    bwd = lambda i: jnp.where(i < ncc, ncc - 1 - i, nch + ncc - 1 - i)

    def chunk_specs(row):
        return [pl.BlockSpec((c, GLA_KW), lambda i: (row(i), OFF_GQ // GLA_KW)),
                pl.BlockSpec((c, GLA_KW), lambda i: (row(i), OFF_GK // GLA_KW)),
                pl.BlockSpec((c, GLA_VW), lambda i: (row(i), OFF_GV // GLA_VW)),
                pl.BlockSpec((c, LANES), lambda i: (row(i), 0))]

    o_shape = jax.ShapeDtypeStruct((n, GLA_VW), F32)
    return pl.pallas_call(
        _gla_kernel,
        grid=(nch,),
        in_specs=chunk_specs(fwd) + chunk_specs(bwd) + [
            pl.BlockSpec((2, LANES, GLA_KW), lambda i: (0, 0, 0)),
            pl.BlockSpec((2, 1, GLA_KW), lambda i: (0, 0, 0))],
        out_specs=[pl.BlockSpec((c, GLA_VW), lambda i: (fwd(i), 0)),
                   pl.BlockSpec((c, GLA_VW), lambda i: (bwd(i), 0))],
        out_shape=[o_shape, o_shape],
        scratch_shapes=[pltpu.VMEM((2, GLA_HEADS, GLA_DV, GLA_DK), F32)],
        compiler_params=_cp(("arbitrary",)),
        name="gla_mix",
    )(z, z, z, lr, z, z, z, lr, w2pad, bias)


def _gla_post_kernel(of_ref, ob_ref, r_ref, g_ref, y_ref):
    for h in range(GLA_HEADS):
        vs = slice(h * GLA_DV, (h + 1) * GLA_DV)
        o = of_ref[:, vs] + ob_ref[:, vs]
        rn = lax.rsqrt(jnp.mean(o * o, axis=-1, keepdims=True) + EPS)
        r = r_ref[:, vs].astype(F32)
        y_ref[:, vs] = ((o * rn) * g_ref[0, :, vs] * (r * _sigmoid(r))).astype(BF16)


def gla_post(o_f, o_b, z, norm_g, l):
    n = z.shape[0]
    tm = _pick(n, (1056, 640, 256))
    return pl.pallas_call(
        _gla_post_kernel,
        grid=(n // tm,),
        in_specs=[pl.BlockSpec((tm, GLA_VW), lambda i: (i, 0)),
                  pl.BlockSpec((tm, GLA_VW), lambda i: (i, 0)),
                  pl.BlockSpec((tm, GLA_VW), lambda i: (i, OFF_GR // GLA_VW)),
                  pl.BlockSpec((1, 1, GLA_VW), lambda i: (l, 0, 0))],
        out_specs=pl.BlockSpec((tm, GLA_VW), lambda i: (i, 0)),
        out_shape=jax.ShapeDtypeStruct((n, GLA_VW), BF16),
        compiler_params=_cp(("arbitrary",)),
        name="gla_post",
    )(o_f, o_b, z, norm_g.reshape(DEPTH, 1, GLA_VW))


def _att_prep_kernel(q_ref, k_ref, v_ref, cos_ref, sin_ref, qo_ref, ko_ref, vo_ref):
    cos = cos_ref[...]
    sin = sin_ref[...]
    lane = lax.broadcasted_iota(jnp.int32, (1, LANES), 1)
    first_half = (lane % ATT_HEAD_DIM) < (ATT_HEAD_DIM // 2)
    low_head = lane < ATT_HEAD_DIM

    def rope(xc):
        rot = jnp.where(first_half, pltpu.roll(xc, LANES - ATT_HEAD_DIM // 2, 1), pltpu.roll(xc, ATT_HEAD_DIM // 2, 1))
        return xc * cos + rot * sin

    def dup(xc, parity):
        sw = pltpu.roll(xc, ATT_HEAD_DIM, 1)
        return jnp.where(low_head, xc, sw) if parity == 0 else jnp.where(low_head, sw, xc)

    for cidx in range(ATT_QW // LANES):
        cs = slice(cidx * LANES, (cidx + 1) * LANES)
        qo_ref[:, cs] = (rope(q_ref[:, cs].astype(F32)) * (ATT_HEAD_DIM ** -0.5)).astype(BF16)
    for kh in range(ATT_KV_HEADS):
        ps = slice((kh // 2) * LANES, (kh // 2 + 1) * LANES)
        os_ = slice(kh * LANES, (kh + 1) * LANES)
        ko_ref[:, os_] = dup(rope(k_ref[:, ps].astype(F32)), kh % 2).astype(BF16)
        vo_ref[:, os_] = dup(v_ref[:, ps].astype(F32), kh % 2).astype(BF16)


def att_prep(z, cos_t, sin_t):
    n = z.shape[0]
    tm = 256
    kw = ATT_KV_HEADS * LANES
    return pl.pallas_call(
        _att_prep_kernel,
        grid=(n // tm,),
        in_specs=[pl.BlockSpec((tm, ATT_QW), lambda i: (i, OFF_AQ // ATT_QW)),
                  pl.BlockSpec((tm, ATT_KVW), lambda i: (i, OFF_AK // ATT_KVW)),
                  pl.BlockSpec((tm, ATT_KVW), lambda i: (i, OFF_AV // ATT_KVW)),
                  pl.BlockSpec((tm, LANES), lambda i: (i, 0)),
                  pl.BlockSpec((tm, LANES), lambda i: (i, 0))],
        out_specs=[pl.BlockSpec((tm, ATT_QW), lambda i: (i, 0)),
                   pl.BlockSpec((tm, kw), lambda i: (i, 0)),
                   pl.BlockSpec((tm, kw), lambda i: (i, 0))],
        out_shape=[jax.ShapeDtypeStruct((n, ATT_QW), BF16),
                   jax.ShapeDtypeStruct((n, kw), BF16),
                   jax.ShapeDtypeStruct((n, kw), BF16)],
        compiler_params=_cp(("arbitrary",)),
        name="att_prep",
    )(z, z, z, cos_t, sin_t)


def _att_heads(q_ref, k_all, v_all, bias, sink_ref, o_ref):
    lane = lax.broadcasted_iota(jnp.int32, (1, LANES), 1)
    low_head = lane < ATT_HEAD_DIM
    grp = ATT_HEADS // ATT_KV_HEADS
    blk = q_ref.shape[0]
    for kh in range(ATT_KV_HEADS):
        k2 = k_all[:, kh * LANES:(kh + 1) * LANES]
        v2 = v_all[:, kh * LANES:(kh + 1) * LANES]
        qs, sinks = [], []
        for j in range(grp):
            h = kh * grp + j
            qp = q_ref[:, (h // 2) * LANES:(h // 2 + 1) * LANES]
            qs.append(jnp.where(low_head if h % 2 == 0 else jnp.logical_not(low_head), qp, jnp.zeros_like(qp)))
            sinks.append(jnp.full((blk, 1), sink_ref[h], F32))
        s = _dot_nt(jnp.concatenate(qs, axis=0), k2)
        if bias is not None:
            s = s + jnp.concatenate([bias] * grp, axis=0)
        sk = jnp.concatenate(sinks, axis=0)
        m = jnp.maximum(jnp.max(s, axis=-1, keepdims=True), sk)
        p = jnp.exp(s - m)
        den = jnp.sum(p, axis=-1, keepdims=True) + jnp.exp(sk - m)
        o = _dot(p.astype(BF16), v2) * (1.0 / den)
        for pr in range(grp // 2):
            pair = kh * (grp // 2) + pr
            even, odd = o[(2 * pr) * blk:(2 * pr + 1) * blk], o[(2 * pr + 1) * blk:(2 * pr + 2) * blk]
            o_ref[:, pair * LANES:(pair + 1) * LANES] = jnp.where(low_head, even, odd).astype(BF16)


def _att_kernel(sink_ref, q_ref, kp_ref, kc_ref, kn_ref, kx_ref, vp_ref, vc_ref, vn_ref, vx_ref, o_ref, *, seq):
    i = pl.program_id(0)
    blk = ATT_BLOCK
    k_all = jnp.concatenate([kp_ref[...], kc_ref[...], kn_ref[...], kx_ref[...]], axis=0)
    v_all = jnp.concatenate([vp_ref[...], vc_ref[...], vn_ref[...], vx_ref[...]], axis=0)
    nctx = kx_ref.shape[0]
    qpos = i * blk + lax.broadcasted_iota(jnp.int32, (blk, 1), 0)
    kpos = (i - 1) * blk + lax.broadcasted_iota(jnp.int32, (1, 3 * blk), 1)
    valid = (jnp.abs(kpos - qpos) <= WINDOW) & (kpos >= 0) & (kpos < seq)
    bias = jnp.concatenate([jnp.where(valid, 0.0, -jnp.inf), jnp.zeros((blk, nctx), F32)], axis=1)
    _att_heads(q_ref, k_all, v_all, bias, sink_ref, o_ref)


def _att_ctx_kernel(sink_ref, q_ref, kx_ref, vx_ref, o_ref):
    _att_heads(q_ref, kx_ref[...], vx_ref[...], None, sink_ref, o_ref)


def attention(qr, k2, v2, sink, ctx_len):
    n = qr.shape[0]
    blk = ATT_BLOCK
    seq = n - ctx_len
    nb, cb = seq // blk, ctx_len // blk
    kw = ATT_KV_HEADS * LANES
    prev = lambda i, s: (jnp.maximum(i - 1, 0) + cb, 0)
    cur = lambda i, s: (i + cb, 0)
    nxt = lambda i, s: (jnp.minimum(i + 1, nb - 1) + cb, 0)
    cx = lambda i, s: (0, 0)
    band = [pl.BlockSpec((blk, kw), prev), pl.BlockSpec((blk, kw), cur), pl.BlockSpec((blk, kw), nxt),
            pl.BlockSpec((ctx_len, kw), cx)]
    y_lat = pl.pallas_call(
        functools.partial(_att_kernel, seq=seq),
        grid_spec=pltpu.PrefetchScalarGridSpec(
            num_scalar_prefetch=1, grid=(nb,),
            in_specs=[pl.BlockSpec((blk, ATT_QW), cur)] + band + band,
            out_specs=pl.BlockSpec((blk, ATT_QW), lambda i, s: (i, 0))),
        out_shape=jax.ShapeDtypeStruct((seq, ATT_QW), BF16),
        compiler_params=_cp(("arbitrary",)),
        name="att_window",
    )(sink, qr, k2, k2, k2, k2, v2, v2, v2, v2)
    y_ctx = pl.pallas_call(
        _att_ctx_kernel,
        grid_spec=pltpu.PrefetchScalarGridSpec(
            num_scalar_prefetch=1, grid=(cb,),
            in_specs=[pl.BlockSpec((blk, ATT_QW), lambda i, s: (i, 0)),
                      pl.BlockSpec((ctx_len, kw), cx), pl.BlockSpec((ctx_len, kw), cx)],
            out_specs=pl.BlockSpec((blk, ATT_QW), lambda i, s: (i, 0))),
        out_shape=jax.ShapeDtypeStruct((ctx_len, ATT_QW), BF16),
        compiler_params=_cp(("arbitrary",)),
        name="att_context",
    )(sink, qr, k2, v2)
    return jnp.concatenate([y_ctx, y_lat], axis=0)


def _merge_kernel(a1_ref, a2_ref, a3_ref, w1_ref, w2_ref, w3_ref, g1_ref, g2_ref, g3_ref, o_ref):
    acc = _sigmoid(g1_ref[...].astype(F32)) * _dot(a1_ref[...], w1_ref[...].astype(BF16))
    acc = acc + _sigmoid(g2_ref[...].astype(F32)) * _dot(a2_ref[...], w2_ref[...].astype(BF16))
    acc = acc + _sigmoid(g3_ref[...].astype(F32)) * _dot(a3_ref[...], w3_ref[...].astype(BF16))
    o_ref[...] = acc.astype(BF16)


def merge_branches(y_s5, y_gla, y_att, z, w_s5, w_gla, w_att, l):
    n = z.shape[0]
    tm = _pick(n, (528, 640, 256))
    tn = 512
    a_spec = pl.BlockSpec((tm, S5_WIDTH), lambda j, i: (i, 0))
    w_spec = pl.BlockSpec((None, S5_WIDTH, tn), lambda j, i: (l, 0, j))

    def g_spec(b):
        return pl.BlockSpec((tm, tn), lambda j, i: (i, (OFF_GATE + b * D_MODEL) // tn + j))

    return pl.pallas_call(
        _merge_kernel,
        grid=(D_MODEL // tn, n // tm),
        in_specs=[a_spec, a_spec, a_spec, w_spec, w_spec, w_spec, g_spec(0), g_spec(1), g_spec(2)],
        out_specs=pl.BlockSpec((tm, tn), lambda j, i: (i, j)),
        out_shape=jax.ShapeDtypeStruct((n, D_MODEL), BF16),
        compiler_params=_cp(("arbitrary", "arbitrary")),
        name="merge",
    )(y_s5, y_gla, y_att, w_s5, w_gla, w_att, z, z, z)


def _out_kernel(m_ref, w_ref, x_ref, mod_ref, g_ref, rt_ref, x1_ref, h2_ref, lg_ref, *, ctx_len, tm):
    i = pl.program_id(0)
    is_ctx = (i * tm + lax.broadcasted_iota(jnp.int32, (tm, 1), 0)) < ctx_len
    x1 = x_ref[...] + _row_mods(mod_ref, 2, is_ctx) * _dot(m_ref[...], w_ref[...])
    x1_ref[...] = x1
    r = lax.rsqrt(jnp.mean(x1 * x1, axis=-1, keepdims=True) + EPS)
    h2 = ((x1 * r) * g_ref[0]) * (1.0 + _row_mods(mod_ref, 4, is_ctx)) + _row_mods(mod_ref, 3, is_ctx)
    h_hi = h2.astype(BF16)
    h2_ref[...] = h_hi
    h_lo = (h2 - h_hi.astype(F32)).astype(BF16)
    rt = rt_ref[...]
    r_hi = rt.astype(BF16)
    r_lo = (rt - r_hi.astype(F32)).astype(BF16)
    lg_ref[...] = _dot(h_hi, r_hi) + (_dot(h_lo, r_hi) + _dot(h_hi, r_lo))


def out_projection(m, w_out_bf, xs, mods, norm_g, router_pad, l, ctx_len):
    n = xs.shape[0]
    tm = _pick(n, (528, 640, 256))
    once = pl.Buffered(1)
    return pl.pallas_call(
        functools.partial(_out_kernel, ctx_len=ctx_len, tm=tm),
        grid=(n // tm,),
        in_specs=[pl.BlockSpec((tm, D_MODEL), lambda i: (i, 0)),
                  pl.BlockSpec((None, D_MODEL, D_MODEL), lambda i: (l, 0, 0), pipeline_mode=once),
                  pl.BlockSpec((tm, D_MODEL), lambda i: (i, 0)),
                  pl.BlockSpec((None, 8, N_MOD * D_MODEL), lambda i: (l, 0, 0)),
                  pl.BlockSpec((1, 1, D_MODEL), lambda i: (l, 0, 0)),
                  pl.BlockSpec((None, D_MODEL, LANES), lambda i: (l, 0, 0), pipeline_mode=once)],
        out_specs=[pl.BlockSpec((tm, D_MODEL), lambda i: (i, 0)),
                   pl.BlockSpec((tm, D_MODEL), lambda i: (i, 0)),
                   pl.BlockSpec((tm, LANES), lambda i: (i, 0))],
        out_shape=[jax.ShapeDtypeStruct((n, D_MODEL), F32),
                   jax.ShapeDtypeStruct((n, D_MODEL), BF16),
                   jax.ShapeDtypeStruct((n, LANES), F32)],
        compiler_params=_cp(("arbitrary",)),
        name="out_projection",
    )(m, w_out_bf, xs, mods, norm_g.reshape(DEPTH, 1, D_MODEL), router_pad)


def _route_kernel(lg_ref, aff_ref, slot_ref, slot_t_ref, s0_ref, cnt_ref, aff_t_scr, *, ctx_len, n):
    tt = TOK_TILE
    lane = lax.broadcasted_iota(jnp.int32, (1, LANES), 1)
    is_expert = lane < N_EXPERTS

    def softmax_tile(b, _):
        r0 = pl.multiple_of(b * tt, tt)
        x = jnp.where(is_expert, lg_ref[pl.ds(r0, tt), :], -jnp.inf)
        e = jnp.exp(x - jnp.max(x, axis=-1, keepdims=True))
        aff = e / jnp.sum(e, axis=-1, keepdims=True)
        aff_ref[pl.ds(r0, tt), :] = aff
        aff_t_scr[:, pl.ds(r0, tt)] = jnp.transpose(aff)[0:N_EXPERTS, :]
        return 0

    lax.fori_loop(0, n // tt, softmax_tile, 0)

    ri = lax.broadcasted_iota(jnp.int32, (tt, tt), 0)
    ci = lax.broadcasted_iota(jnp.int32, (tt, tt), 1)
    before = jnp.where(ci < ri, 1.0, 0.0).astype(BF16)
    er = lax.broadcasted_iota(jnp.int32, (N_EXPERTS, LANES), 0)
    el = lax.broadcasted_iota(jnp.int32, (N_EXPERTS, LANES), 1)

    def as_row(col):
        return jnp.sum(jnp.where(er == el, jnp.broadcast_to(col, (N_EXPERTS, LANES)), 0), axis=0, keepdims=True)

    def bits_of(b):
        r0 = pl.multiple_of(b * tt, tt)
        return pltpu.bitcast(aff_ref[pl.ds(r0, tt), :], jnp.int32)

    def route_set(b0, b1, slot_base):
        cap = float(EC_CAPACITY * (b1 - b0) * tt // N_EXPERTS)
        def count(pred):
            hit = pred(pltpu.bitcast(aff_t_scr[:, b0 * tt:b1 * tt], jnp.int32))
            return jnp.sum(jnp.where(hit, 1.0, 0.0), axis=-1, keepdims=True)

        def bisect(_, lh):
            lo, hi = lh
            mid = lo + lax.shift_right_logical(hi - lo + 1, 1)
            ok = count(lambda v: v >= mid) >= cap
            return jnp.where(ok, mid, lo), jnp.where(ok, hi, mid - 1)

        lo0 = jnp.zeros((N_EXPERTS, 1), jnp.int32)
        hi0 = jnp.full((N_EXPERTS, 1), 0x7F800000, jnp.int32)
        thr_col, _ = lax.fori_loop(0, 31, bisect, (lo0, hi0))
        need = as_row(cap - count(lambda v: v > thr_col))
        thr = as_row(thr_col)

        def assign(b, carry):
            tie_seen, taken = carry
            r0 = pl.multiple_of(b * tt, tt)
            v = bits_of(b)
            tie = jnp.where(v == thr, 1.0, 0.0)
            tie_rank = _dot(before, tie.astype(BF16)) + tie_seen
            sel = jnp.where((v > thr) | ((v == thr) & (tie_rank < need)), 1.0, 0.0)
            rank = _dot(before, sel.astype(BF16)) + taken
            slot = jnp.where(sel > 0.0, rank + slot_base, -1.0)
            slot_ref[pl.ds(r0, tt), :] = slot.astype(jnp.int32)
            slot_t_ref[:, pl.ds(r0, tt)] = jnp.transpose(slot)[0:N_EXPERTS, :].astype(jnp.int32)
            n_sel = jnp.sum(sel, axis=0, keepdims=True)
            s0_ref[b] = (taken + slot_base).astype(jnp.int32)
            cnt_ref[b] = n_sel.astype(jnp.int32)
            return tie_seen + jnp.sum(tie, axis=0, keepdims=True), taken + n_sel

        zero = jnp.zeros((1, LANES), F32)
        lax.fori_loop(b0, b1, assign, (zero, zero))
        return cap

    cb = ctx_len // tt
    cap_ctx = route_set(0, cb, 0.0)
    route_set(cb, n // tt, cap_ctx)


def route(logits, ctx_len):
    n = logits.shape[0]
    nt = n // TOK_TILE
    full = lambda shape: pl.BlockSpec(shape, lambda i: tuple(0 for _ in shape))
    return pl.pallas_call(
        functools.partial(_route_kernel, ctx_len=ctx_len, n=n),
        grid=(1,),
        in_specs=[full((n, LANES))],
        out_specs=[full((n, LANES)), full((n, LANES)), full((N_EXPERTS, n)), full((nt, 1, LANES)),
                   full((nt, 1, LANES))],
        out_shape=[jax.ShapeDtypeStruct((n, LANES), F32),
                   jax.ShapeDtypeStruct((n, LANES), jnp.int32),
                   jax.ShapeDtypeStruct((N_EXPERTS, n), jnp.int32),
                   jax.ShapeDtypeStruct((nt, 1, LANES), jnp.int32),
                   jax.ShapeDtypeStruct((nt, 1, LANES), jnp.int32)],
        scratch_shapes=[pltpu.VMEM((N_EXPERTS, n), F32)],
        compiler_params=_cp(("arbitrary",)),
        name="route",
    )(logits)


def _window_plan(s0, cnt, rows, align_bits=3):
    a0 = lax.shift_left(lax.shift_right_logical(s0, align_bits), align_bits)
    nwin = jnp.where(cnt > 0, lax.shift_right_logical(s0 - a0 + cnt + MOE_WIN - 1, MOE_WIN.bit_length() - 1), 0)
    return a0, nwin, rows - MOE_WIN


GATHER_GROUP = 4
BF16_ROWS_LOG2 = 4


def _gather_kernel(s0_ref, cnt_ref, h_ref, slot_ref, xs_ref, *, rows):
    g = pl.program_id(0)
    t = pl.program_id(1)

    @pl.when(t == 0)
    def _():
        xs_ref[...] = jnp.zeros_like(xs_ref)

    def window(i, j, a0, last):
        start = a0 + j * MOE_WIN
        base = pl.multiple_of(jnp.minimum(start, last), 1 << BF16_ROWS_LOG2)
        want = base + lax.broadcasted_iota(jnp.int32, (MOE_WIN, 1), 0)
        want = jnp.where(want >= start, want, -2)
        return base, jnp.where(slot_ref[i] == want, 1.0, 0.0).astype(BF16)

    def add_rows(i, base, new):
        prev = xs_ref[i, pl.ds(base, MOE_WIN), :].astype(F32)
        xs_ref[i, pl.ds(base, MOE_WIN), :] = (prev + new).astype(BF16)

    plans = [_window_plan(s0_ref[t, g * GATHER_GROUP + i], cnt_ref[t, g * GATHER_GROUP + i], rows, BF16_ROWS_LOG2)
             for i in range(GATHER_GROUP)]
    firsts = [window(i, 0, plans[i][0], plans[i][2]) for i in range(GATHER_GROUP)]
    packed = _dot(jnp.concatenate([oh for _, oh in firsts], axis=0), h_ref[...])
    for i in range(GATHER_GROUP):
        add_rows(i, firsts[i][0], packed[i * MOE_WIN:(i + 1) * MOE_WIN])

        def body(j, _, i=i):
            base, onehot = window(i, j, plans[i][0], plans[i][2])
            add_rows(i, base, _dot(onehot, h_ref[...]))
            return 0

        lax.fori_loop(1, plans[i][1], body, 0)


def moe_gather(h2, slot_t, s0, cnt, rows):
    n = h2.shape[0]
    nt = n // TOK_TILE
    gg = GATHER_GROUP
    return pl.pallas_call(
        functools.partial(_gather_kernel, rows=rows),
        grid_spec=pltpu.PrefetchScalarGridSpec(
            num_scalar_prefetch=2, grid=(N_EXPERTS // gg, nt),
            in_specs=[pl.BlockSpec((TOK_TILE, D_MODEL), lambda g, t, a, b: (t, 0)),
                      pl.BlockSpec((gg, 1, TOK_TILE), lambda g, t, a, b: (g, 0, t))],
            out_specs=pl.BlockSpec((gg, rows, D_MODEL), lambda g, t, a, b: (g, 0, 0))),
        out_shape=jax.ShapeDtypeStruct((N_EXPERTS, rows, D_MODEL), BF16),
        compiler_params=_cp(("arbitrary", "arbitrary")),
        name="moe_gather",
    )(s0, cnt, h2, slot_t.reshape(N_EXPERTS, 1, n))


def _ffn_kernel(xs_ref, wg_ref, wu_ref, wd_ref, y_ref, act_scr):
    f = pl.program_id(1)
    nf = EXPERT_FF // FF_TILE

    @pl.when(f < nf)
    def _():
        x = xs_ref[0]
        a = _dot(x, wg_ref[0].astype(BF16))
        u = _dot(x, wu_ref[0].astype(BF16))
        act_scr[:, pl.ds(pl.multiple_of(f * FF_TILE, FF_TILE), FF_TILE)] = ((a * _sigmoid(a)) * u).astype(BF16)

    @pl.when(f >= nf)
    def _():
        y_ref[0] = _dot(act_scr[...], wd_ref[0].astype(BF16)).astype(BF16)


def moe_ffn(xs, w_gate, w_up, w_down, l):
    rows = xs.shape[1]
    nf = EXPERT_FF // FF_TILE
    up_idx = lambda e, f: (l, e, 0, jnp.minimum(f, nf - 1))
    return pl.pallas_call(
        _ffn_kernel,
        grid=(N_EXPERTS, nf + D_MODEL // FF_TILE),
        in_specs=[pl.BlockSpec((1, rows, D_MODEL), lambda e, f: (e, 0, 0)),
                  pl.BlockSpec((None, 1, D_MODEL, FF_TILE), up_idx),
                  pl.BlockSpec((None, 1, D_MODEL, FF_TILE), up_idx),
                  pl.BlockSpec((None, 1, EXPERT_FF, FF_TILE), lambda e, f: (l, e, 0, jnp.maximum(f - nf, 0)))],
        out_specs=pl.BlockSpec((1, rows, FF_TILE), lambda e, f: (e, 0, jnp.maximum(f - nf, 0))),
        out_shape=jax.ShapeDtypeStruct((N_EXPERTS, rows, D_MODEL), BF16),
        scratch_shapes=[pltpu.VMEM((rows, EXPERT_FF), BF16)],
        compiler_params=_cp(("arbitrary", "arbitrary")),
        name="moe_ffn",
    )(xs, w_gate, w_up, w_down)


MORE_WINS = TOK_TILE // MOE_WIN


def _combine_kernel(s0_ref, cnt_ref, x_ref, slot_ref, aff_ref, mod_ref, g_ref, y_hbm, o_ref, buf0, bufx, sem0, semx,
                    *, rows, ctx_len, final):
    t = pl.program_id(0)
    nt = pl.num_programs(0)
    tt = TOK_TILE

    def first_copy(tile, e, par):
        a0, _, last = _window_plan(s0_ref[tile, e], cnt_ref[tile, e], rows)
        base = pl.multiple_of(jnp.minimum(a0, last), 8)
        cp = pltpu.make_async_copy(y_hbm.at[e, pl.ds(base, MOE_WIN)],
                                   buf0.at[par, pl.ds(e * MOE_WIN, MOE_WIN)], sem0.at[par, e])
        return base, a0, cp

    def more_copy(e, j, a0, last):
        base = pl.multiple_of(jnp.minimum(a0 + j * MOE_WIN, last), 8)
        k = e * MORE_WINS + j - 1
        return base, pltpu.make_async_copy(y_hbm.at[e, pl.ds(base, MOE_WIN)], bufx.at[k], semx.at[k])

    @pl.when(t == 0)
    def _():
        for e in range(N_EXPERTS):
            first_copy(0, e, 0)[2].start()

    @pl.when(t + 1 < nt)
    def _():
        for e in range(N_EXPERTS):
            first_copy(t + 1, e, (t + 1) % 2)[2].start()

    for e in range(N_EXPERTS):
        a0, nwin, last = _window_plan(s0_ref[t, e], cnt_ref[t, e], rows)

        def start(j, _, e=e, a0=a0, last=last):
            more_copy(e, j, a0, last)[1].start()
            return 0

        lax.fori_loop(1, nwin, start, 0)

    par = t % 2
    lane = lax.broadcasted_iota(jnp.int32, (1, LANES), 1)
    left = lane < MOE_WIN
    offs = jnp.where(left, lane, lane - MOE_WIN)
    cols_hi, cols_lo = [], []
    for c in range(N_EXPERTS * MOE_WIN // LANES):
        e0, e1 = 2 * c, 2 * c + 1
        b0, a00, cp0 = first_copy(t, e0, par)
        b1, a01, cp1 = first_copy(t, e1, par)
        cp0.wait()
        cp1.wait()
        want = jnp.where(left, b0, b1) + offs
        want = jnp.where(want >= jnp.where(left, a00, a01), want, -2)
        picked = jnp.where(left, slot_ref[:, e0:e0 + 1], slot_ref[:, e1:e1 + 1]) == want
        gate = jnp.where(picked, jnp.where(left, aff_ref[:, e0:e0 + 1], aff_ref[:, e1:e1 + 1]), 0.0)
        g_hi = gate.astype(BF16)
        cols_hi.append(g_hi)
        cols_lo.append((gate - g_hi.astype(F32)).astype(BF16))
    y0 = buf0[par]
    o_ref[...] = _dot(jnp.concatenate(cols_hi, axis=-1), y0) + _dot(jnp.concatenate(cols_lo, axis=-1), y0)

    for e in range(N_EXPERTS):
        a0, nwin, last = _window_plan(s0_ref[t, e], cnt_ref[t, e], rows)

        def take(j, _, e=e, a0=a0, last=last):
            base, cp = more_copy(e, j, a0, last)
            cp.wait()
            want = base + lax.broadcasted_iota(jnp.int32, (1, MOE_WIN), 1)
            want = jnp.where(want >= a0 + j * MOE_WIN, want, -2)
            onehot = jnp.where(slot_ref[:, e:e + 1] == want, 1.0, 0.0).astype(BF16)
            o_ref[...] += aff_ref[:, e:e + 1] * _dot(onehot, bufx[e * MORE_WINS + j - 1])
            return 0

        lax.fori_loop(1, nwin, take, 0)

    is_ctx = (t * tt + lax.broadcasted_iota(jnp.int32, (tt, 1), 0)) < ctx_len
    x2 = x_ref[...] + _row_mods(mod_ref, 5, is_ctx) * o_ref[...]
    if final:
        r = lax.rsqrt(jnp.mean(x2 * x2, axis=-1, keepdims=True) + EPS)
        x2 = (x2 * r) * g_ref[...]
    o_ref[...] = x2


def moe_combine(x1, slot, aff, s0, cnt, mods, y, final_g, l, ctx_len, final):
    n = x1.shape[0]
    rows = y.shape[1]
    tt = TOK_TILE
    nbuf = N_EXPERTS * MORE_WINS
    return pl.pallas_call(
        functools.partial(_combine_kernel, rows=rows, ctx_len=ctx_len, final=final),
        grid_spec=pltpu.PrefetchScalarGridSpec(
            num_scalar_prefetch=2, grid=(n // tt,),
            in_specs=[pl.BlockSpec((tt, D_MODEL), lambda t, a, b: (t, 0)),
                      pl.BlockSpec((tt, LANES), lambda t, a, b: (t, 0)),
                      pl.BlockSpec((tt, LANES), lambda t, a, b: (t, 0)),
                      pl.BlockSpec((None, 8, N_MOD * D_MODEL), lambda t, a, b: (l, 0, 0)),
                      pl.BlockSpec((1, D_MODEL), lambda t, a, b: (0, 0)),
                      pl.BlockSpec(memory_space=pl.ANY)],
            out_specs=pl.BlockSpec((tt, D_MODEL), lambda t, a, b: (t, 0)),
            scratch_shapes=[pltpu.VMEM((2, N_EXPERTS * MOE_WIN, D_MODEL), BF16),
                            pltpu.VMEM((nbuf, MOE_WIN, D_MODEL), BF16),
                            pltpu.SemaphoreType.DMA((2, N_EXPERTS)),
                            pltpu.SemaphoreType.DMA((nbuf,))]),
        out_shape=jax.ShapeDtypeStruct((n, D_MODEL), F32),
        compiler_params=_cp(("arbitrary",)),
        name="moe_combine",
    )(s0, cnt, x1, slot, aff, mods, final_g.reshape(1, D_MODEL), y)


def rope_tables(seq, ctx_len):
    pos = jnp.arange(seq)
    row = (pos // GRID_W).astype(F32)
    col = (pos % GRID_W).astype(F32)
    n_freq = ATT_HEAD_DIM // 4
    inv_freq = ROPE_BASE ** (-jnp.arange(n_freq, dtype=F32) / n_freq)
    ang = jnp.concatenate([row[:, None] * inv_freq, col[:, None] * inv_freq], axis=-1)
    cos, sin = jnp.cos(ang), jnp.sin(ang)
    cos_t = jnp.concatenate([jnp.ones((ctx_len, LANES), F32), jnp.tile(cos, (1, 4))], axis=0)
    sin_t = jnp.concatenate([jnp.zeros((ctx_len, LANES), F32), jnp.tile(jnp.concatenate([-sin, sin], -1), (1, 2))],
                            axis=0)
    return cos_t, sin_t


def kernel(x, c, ctx, c_ctx, ada_w, ada_b, norm1_g, norm2_g, w_in, s5_lam_re, s5_lam_im, s5_log_dt,
           s5_b_re, s5_b_im, s5_c_re, s5_c_im, s5_d, s5_w_glu, gla_w1, gla_w2, gla_b, gla_norm_g,
           attn_sink, w_branch_s5, w_branch_gla, w_branch_attn, w_out, moe_router, moe_w_gate,
           moe_w_up, moe_w_down, final_g):
    seq, ctx_len = x.shape[1], ctx.shape[1]
    n = seq + ctx_len
    assert x.shape[0] == 1 and ctx_len % TOK_TILE == 0 and seq % TOK_TILE == 0
    xs = jnp.concatenate([ctx[0], x[0]], axis=0)
    c8 = jnp.zeros((8, D_MODEL), F32).at[0].set(c[0]).at[1].set(c_ctx)
    mods = ada_mods(c8, ada_w, ada_b)
    cos_t, sin_t = rope_tables(seq, ctx_len)
    w_out_bf = w_out.astype(BF16)
    w_in_bf = w_in.astype(BF16)
    router_pad = jnp.pad(moe_router, ((0, 0), (0, 0), (0, LANES - N_EXPERTS)))
    rows = EC_CAPACITY * n // N_EXPERTS
    for l in range(DEPTH):
        w1cat = jnp.pad(jnp.concatenate([gla_w1[l, 0], gla_w1[l, 1]], axis=-1),
                        ((0, 0), (0, LANES - 2 * GLA_RANK)))
        w2pad = jnp.zeros((2, LANES, GLA_KW), F32)
        w2pad = w2pad.at[0, 0:GLA_RANK].set(gla_w2[l, 0]).at[1, GLA_RANK:2 * GLA_RANK].set(gla_w2[l, 1])
        zu, z, lr = in_projection(xs, norm1_g, mods, w_in_bf, w1cat, l, ctx_len)
        s5p = s5_params(s5_lam_re[l], s5_lam_im[l], s5_log_dt[l], s5_b_re[l], s5_b_im[l])
        y_s5 = s5_glu(s5_mix(zu, s5p, s5_c_re[l], s5_c_im[l], s5_d[l], ctx_len), s5_w_glu, l)
        gla_bias = gla_b[l].reshape(2, 1, GLA_KW)
        o_f, o_b = gla_mix(z, lr, w2pad, gla_bias, ctx_len)
        y_gla = gla_post(o_f, o_b, z, gla_norm_g, l)
        qr, k2, v2 = att_prep(z, cos_t, sin_t)
        y_att = attention(qr, k2, v2, attn_sink[l], ctx_len)
        m = merge_branches(y_s5, y_gla, y_att, z, w_branch_s5, w_branch_gla, w_branch_attn, l)
        x1, h2, logits = out_projection(m, w_out_bf, xs, mods, norm2_g, router_pad, l, ctx_len)
        aff, slot, slot_t, s0, cnt = route(logits, ctx_len)
        s0, cnt = s0[:, 0, :N_EXPERTS], cnt[:, 0, :N_EXPERTS]
        xe = moe_gather(h2, slot_t, s0, cnt, rows)
        ye = moe_ffn(xe, moe_w_gate, moe_w_up, moe_w_down, l)
        xs = moe_combine(x1, slot, aff, s0, cnt, mods, ye, final_g, l, ctx_len, l == DEPTH - 1)
    return xs[ctx_len:][None]
```

```python
import functools
import math

import jax
import jax.numpy as jnp
from jax import lax
from jax.experimental import pallas as pl
from jax.experimental.pallas import tpu as pltpu

D_MODEL = 2048
DEPTH = 2
GRID_W = 64
EPS = 1e-6
N_MOD = 6
S5_WIDTH = 1024
S5_GROUP_CH = 16
S5_GROUPS = S5_WIDTH // S5_GROUP_CH
S5_STATE = 64
GLA_HEADS = 4
GLA_DK = 128
GLA_DV = 256
GLA_KW = GLA_HEADS * GLA_DK
GLA_VW = GLA_HEADS * GLA_DV
GLA_RANK = 16
GLA_TAU = 16.0
GLA_CHUNK = 64
ATT_HEADS = 16
ATT_KV_HEADS = 4
ATT_HEAD_DIM = 64
ATT_QW = ATT_HEADS * ATT_HEAD_DIM
ATT_KVW = ATT_KV_HEADS * ATT_HEAD_DIM
WINDOW = 128
ATT_BLOCK = 128
ROPE_BASE = 10000.0
N_BRANCH = 3
IN_SPLITS = (S5_WIDTH, GLA_KW, GLA_KW, GLA_VW, GLA_VW, ATT_QW, ATT_KVW, ATT_KVW, N_BRANCH * D_MODEL)
IN_WIDTH = sum(IN_SPLITS)
N_EXPERTS = 16
EXPERT_FF = 2048
EC_CAPACITY = 2

OFF_GQ = 0
OFF_GK = OFF_GQ + GLA_KW
OFF_GV = OFF_GK + GLA_KW
OFF_GR = OFF_GV + GLA_VW
OFF_AQ = OFF_GR + GLA_VW
OFF_AK = OFF_AQ + ATT_QW
OFF_AV = OFF_AK + ATT_KVW
OFF_GATE = OFF_AV + ATT_KVW

LANES = 128
S5_T = 16
S5_PAIRS = S5_GROUPS // 2
S5_BLOCKS = S5_WIDTH // LANES
S5_BLK_GROUPS = LANES // S5_GROUP_CH
S5_BLK_PAIRS = S5_BLK_GROUPS // 2
S5_BLK_STATE = S5_BLK_GROUPS * S5_STATE
TOK_TILE = 256
MOE_WIN = 64
FF_TILE = 512
VMEM_MB = 56

F32 = jnp.float32
BF16 = jnp.bfloat16
HI = lax.Precision.HIGHEST


def _cp(sem, mb=VMEM_MB):
    return pltpu.CompilerParams(dimension_semantics=sem, vmem_limit_bytes=mb << 20)


def _dot(a, b):
    return jnp.dot(a, b, preferred_element_type=F32)


def _dot_nt(a, b):
    return lax.dot_general(a, b, (((1,), (1,)), ((), ())), preferred_element_type=F32)


def _dot_tn(a, b):
    return lax.dot_general(a, b, (((0,), (0,)), ((), ())), preferred_element_type=F32)


def _pick(n, cands):
    for c in cands:
        if n % c == 0:
            return c
    raise ValueError(f"no tile for {n} in {cands}")


def _sigmoid(x):
    return 1.0 / (1.0 + jnp.exp(-x))


def _row_mods(mod_ref, k, is_ctx):
    lo, hi = k * D_MODEL, (k + 1) * D_MODEL
    return jnp.where(is_ctx, mod_ref[1:2, lo:hi], mod_ref[0:1, lo:hi])


def _ada_kernel(c_ref, w_ref, b_ref, o_ref):
    cc = c_ref[...]
    o_ref[0] = jnp.dot(cc * _sigmoid(cc), w_ref[0], preferred_element_type=F32, precision=HI) + b_ref[0]


def ada_mods(c8, ada_w, ada_b):
    tn = 1024
    width = N_MOD * D_MODEL
    return pl.pallas_call(
        _ada_kernel,
        grid=(DEPTH, width // tn),
        in_specs=[pl.BlockSpec((8, D_MODEL), lambda l, j: (0, 0)),
                  pl.BlockSpec((1, D_MODEL, tn), lambda l, j: (l, 0, j)),
                  pl.BlockSpec((1, 1, tn), lambda l, j: (l, 0, j))],
        out_specs=pl.BlockSpec((1, 8, tn), lambda l, j: (l, 0, j)),
        out_shape=jax.ShapeDtypeStruct((DEPTH, 8, width), F32),
        compiler_params=_cp(("arbitrary", "arbitrary")),
        name="ada_mods",
    )(c8, ada_w, ada_b.reshape(DEPTH, 1, width))


def _in_kernel(x_ref, g_ref, mod_ref, w_ref, w1_ref, zu_ref, zr_ref, lr_ref, h_scr, *, ctx_len, tm, nu):
    i = pl.program_id(0)
    j = pl.program_id(1)

    @pl.when(j == 0)
    def _():
        x = x_ref[...]
        r = lax.rsqrt(jnp.mean(x * x, axis=-1, keepdims=True) + EPS)
        hn = (x * r) * g_ref[0]
        is_ctx = (i * tm + lax.broadcasted_iota(jnp.int32, (tm, 1), 0)) < ctx_len
        hb = (hn * (1.0 + _row_mods(mod_ref, 1, is_ctx)) + _row_mods(mod_ref, 0, is_ctx)).astype(BF16)
        h_scr[...] = hb
        lr_ref[...] = _dot(hb, w1_ref[...].astype(BF16))

    z = _dot(h_scr[...], w_ref[...].astype(BF16))

    @pl.when(j < nu)
    def _():
        zu_ref[...] = z

    @pl.when(j >= nu)
    def _():
        zr_ref[...] = z.astype(BF16)


def in_projection(xs, norm_g, mods, w_in, w1cat, l, ctx_len):
    n = xs.shape[0]
    tm = _pick(n, (1056, 640, 256))
    tn = 512
    nu = S5_WIDTH // tn
    return pl.pallas_call(
        functools.partial(_in_kernel, ctx_len=ctx_len, tm=tm, nu=nu),
        grid=(n // tm, IN_WIDTH // tn),
        in_specs=[pl.BlockSpec((tm, D_MODEL), lambda i, j: (i, 0)),
                  pl.BlockSpec((1, 1, D_MODEL), lambda i, j: (l, 0, 0)),
                  pl.BlockSpec((None, 8, N_MOD * D_MODEL), lambda i, j: (l, 0, 0)),
                  pl.BlockSpec((None, D_MODEL, tn), lambda i, j: (l, 0, j)),
                  pl.BlockSpec((D_MODEL, LANES), lambda i, j: (0, 0))],
        out_specs=[pl.BlockSpec((tm, tn), lambda i, j: (i, jnp.minimum(j, nu - 1))),
                   pl.BlockSpec((tm, tn), lambda i, j: (i, jnp.maximum(j - nu, 0))),
                   pl.BlockSpec((tm, LANES), lambda i, j: (i, 0))],
        out_shape=[jax.ShapeDtypeStruct((n, S5_WIDTH), F32),
                   jax.ShapeDtypeStruct((n, IN_WIDTH - S5_WIDTH), BF16),
                   jax.ShapeDtypeStruct((n, LANES), F32)],
        scratch_shapes=[pltpu.VMEM((tm, D_MODEL), BF16)],
        compiler_params=_cp(("arbitrary", "arbitrary")),
        name="in_projection",
    )(xs, norm_g.reshape(DEPTH, 1, D_MODEL), mods, w_in, w1cat)


def s5_params(lam_re, lam_im, log_dt, b_re, b_im):
    dt = jnp.exp(log_dt)[..., None]
    mag = jnp.exp(lam_re * dt)
    ab_re = mag * jnp.cos(lam_im * dt)
    ab_im = mag * jnp.sin(lam_im * dt)
    den = lam_re * lam_re + lam_im * lam_im
    nr = ab_re - 1
    f_re = ((nr * lam_re + ab_im * lam_im) / den)[:, :, None, :]
    f_im = ((ab_im * lam_re - nr * lam_im) / den)[:, :, None, :]
    bt_re, bt_im = b_re.transpose(0, 1, 3, 2), b_im.transpose(0, 1, 3, 2)
    bb_re = f_re * bt_re - f_im * bt_im
    bb_im = f_re * bt_im + f_im * bt_re
    tau = jnp.arange(S5_T + 1, dtype=F32)[None, :, None, None]
    pmag = jnp.exp((lam_re * dt)[:, None] * tau)
    ang = (lam_im * dt)[:, None] * tau
    return pmag * jnp.cos(ang), pmag * jnp.sin(ang), bb_re, bb_im


def _s5_chunk_rows(z_ref, nch):
    return jnp.concatenate([z_ref[pl.ds(t, nch, stride=S5_T), :].astype(BF16) for t in range(S5_T)], axis=-1)


def _s5_place(stage, pr, pi, wre_ref, wim_ref, d, im_sign):
    ch, n_st = S5_GROUP_CH, S5_STATE
    for g in range(S5_BLK_GROUPS):
        a_r, a_i = pr[g:g + 1, :], pi[g:g + 1, :]
        w_r, w_i = wre_ref[d, g], wim_ref[d, g]
        stage[g * ch:(g + 1) * ch, g * n_st:(g + 1) * n_st] = a_r * w_r - a_i * w_i
        stage[g * ch:(g + 1) * ch, S5_BLK_STATE + g * n_st:S5_BLK_STATE + (g + 1) * n_st] = (
            im_sign * (a_r * w_i + a_i * w_r))


def _s5_state_in_kernel(z_ref, pre_ref, pim_ref, bre_ref, bim_ref, vre_ref, vim_ref, wq_scr, stage, *, nch):
    @pl.when((pl.program_id(0) == 0) & (pl.program_id(1) == 0))
    def _():
        stage[...] = jnp.zeros_like(stage)

    rev = pl.program_id(1) == 1
    for t in range(S5_T):
        e_t = jnp.where(rev, t, S5_T - 1 - t)
        _s5_place(stage, pre_ref[0, e_t], pim_ref[0, e_t], bre_ref, bim_ref, 0, 1.0)
        wq_scr[t * LANES:(t + 1) * LANES, :] = stage[...].astype(BF16)
    v = _dot(_s5_chunk_rows(z_ref, nch), wq_scr[...])
    vre_ref[0] = v[:, :S5_BLK_STATE]
    vim_ref[0] = v[:, S5_BLK_STATE:]


def _s5_scan_kernel(vre_ref, vim_ref, are_ref, aim_ref, sre_ref, sim_ref, *, nch, ncc):
    rev = pl.program_id(0) == 1
    ar = are_ref[0]
    ai = aim_ref[0]

    def body(i, carry):
        sr, si = carry
        k = jnp.where(rev, jnp.where(i < ncc, ncc - 1 - i, nch + ncc - 1 - i), i)
        sre_ref[0, k] = sr
        sim_ref[0, k] = si
        return ar * sr - ai * si + vre_ref[0, k], ar * si + ai * sr + vim_ref[0, k]

    zero = jnp.zeros((S5_BLOCKS, LANES), F32)
    lax.fori_loop(0, nch, body, (zero, zero))


def _s5_out_kernel(z_ref, sre_ref, sim_ref, pre_ref, pim_ref, bre_ref, bim_ref, cre_ref, cim_ref, d_ref, y_ref,
                   wq_scr, wt_scr, stage, bb_scr, *, nch):
    @pl.when(pl.program_id(0) == 0)
    def _():
        stage[...] = jnp.zeros_like(stage)
        bb_scr[...] = jnp.zeros_like(bb_scr)

    one = jnp.ones((1, S5_STATE), F32)
    zero = jnp.zeros((1, S5_STATE), F32)
    ones_re = jnp.concatenate([one] * S5_BLK_GROUPS, axis=0)
    zeros_im = jnp.concatenate([zero] * S5_BLK_GROUPS, axis=0)
    for d in range(2):
        _s5_place(bb_scr.at[d], ones_re, zeros_im, bre_ref, bim_ref, d, 1.0)
        bb = bb_scr[d]
        bb_hi = bb.astype(BF16)
        bb_lo = (bb - bb_hi.astype(F32)).astype(BF16)
        for tau in range(S5_T + 1):
            _s5_place(stage, pre_ref[d, tau], pim_ref[d, tau], cre_ref, cim_ref, d, -1.0)
            blk = stage[...]
            blk_hi = blk.astype(BF16)
            if tau < S5_T:
                k = S5_T - 1 - tau if d == 0 else tau
                blk_lo = (blk - blk_hi.astype(F32)).astype(BF16)
                kq = _dot_nt(bb_hi, blk_hi) + (_dot_nt(bb_lo, blk_hi) + _dot_nt(bb_hi, blk_lo))
                wt_scr[d, k * LANES:(k + 1) * LANES, :] = kq.astype(BF16)
            if tau > 0:
                t = tau - 1 if d == 0 else S5_T - tau
                wq_scr[d, t * LANES:(t + 1) * LANES, :] = blk_hi
    rows = _s5_chunk_rows(z_ref, nch)
    carried = None
    for d in range(2):
        sp = jnp.concatenate([sre_ref[d], sim_ref[d]], axis=-1).astype(BF16)
        part = _dot_nt(sp, wq_scr[d])
        carried = part if carried is None else carried + part
    for t in range(S5_T):
        acc = z_ref[pl.ds(t, nch, stride=S5_T), :] * d_ref[...] + carried[:, t * LANES:(t + 1) * LANES]
        acc = acc + _dot(rows[:, :LANES * (t + 1)], wt_scr[0, LANES * (S5_T - 1 - t):, :])
        acc = acc + _dot(rows[:, LANES * t:], wt_scr[1, :LANES * (S5_T - t), :])
        y_ref[pl.ds(t, nch, stride=S5_T), :] = acc


def s5_mix(z, params, c_re, c_im, d_skip, ctx_len):
    n = z.shape[0]
    nch, ncc = n // S5_T, ctx_len // S5_T
    p_re, p_im, bb_re, bb_im = params
    st, tl = S5_BLK_STATE, S5_T * LANES
    gb, ch, n_st = S5_BLK_GROUPS, S5_GROUP_CH, S5_STATE
    a_re = p_re[:, S5_T].reshape(2, S5_BLOCKS, st)
    a_im = p_im[:, S5_T].reshape(2, S5_BLOCKS, st)
    v_shape = jax.ShapeDtypeStruct((2, nch, S5_BLOCKS * st), F32)
    pow_spec = pl.BlockSpec((1, S5_T + 1, gb, n_st), lambda q, d: (d, 0, q, 0))
    par_spec = pl.BlockSpec((1, gb, ch, n_st), lambda q, d: (d, q, 0, 0))
    v_re, v_im = pl.pallas_call(
        functools.partial(_s5_state_in_kernel, nch=nch),
        grid=(S5_BLOCKS, 2),
        in_specs=[pl.BlockSpec((n, LANES), lambda q, d: (0, q)), pow_spec, pow_spec, par_spec, par_spec],
        out_specs=[pl.BlockSpec((1, nch, st), lambda q, d: (d, 0, q))] * 2,
        out_shape=[v_shape, v_shape],
        scratch_shapes=[pltpu.VMEM((tl, 2 * st), BF16), pltpu.VMEM((LANES, 2 * st), F32)],
        compiler_params=_cp(("arbitrary", "arbitrary")),
        name="s5_state_in",
    )(z, p_re, p_im, bb_re, bb_im)
    blocked = (2, nch, S5_BLOCKS, st)
    v_spec = pl.BlockSpec((1, nch, S5_BLOCKS, LANES), lambda d, j: (d, 0, 0, j))
    a_spec = pl.BlockSpec((1, S5_BLOCKS, LANES), lambda d, j: (d, 0, j))
    s_shape = jax.ShapeDtypeStruct(blocked, F32)
    s_re, s_im = pl.pallas_call(
        functools.partial(_s5_scan_kernel, nch=nch, ncc=ncc),
        grid=(2, st // LANES),
        in_specs=[v_spec, v_spec, a_spec, a_spec],
        out_specs=[v_spec, v_spec],
        out_shape=[s_shape, s_shape],
        compiler_params=_cp(("arbitrary", "arbitrary")),
        name="s5_scan",
    )(v_re.reshape(blocked), v_im.reshape(blocked), a_re, a_im)
    s_spec = pl.BlockSpec((2, nch, st), lambda q: (0, 0, q))
    pow2_spec = pl.BlockSpec((2, S5_T + 1, gb, n_st), lambda q: (0, 0, q, 0))
    par2_spec = pl.BlockSpec((2, gb, ch, n_st), lambda q: (0, q, 0, 0))
    return pl.pallas_call(
        functools.partial(_s5_out_kernel, nch=nch),
        grid=(S5_BLOCKS,),
        in_specs=[pl.BlockSpec((n, LANES), lambda q: (0, q)), s_spec, s_spec, pow2_spec, pow2_spec,
                  par2_spec, par2_spec, par2_spec, par2_spec, pl.BlockSpec((1, LANES), lambda q: (0, q))],
        out_specs=pl.BlockSpec((n, LANES), lambda q: (0, q)),
        out_shape=jax.ShapeDtypeStruct((n, S5_WIDTH), F32),
        scratch_shapes=[pltpu.VMEM((2, tl, 2 * st), BF16), pltpu.VMEM((2, tl, LANES), BF16),
                        pltpu.VMEM((LANES, 2 * st), F32), pltpu.VMEM((2, LANES, 2 * st), F32)],
        compiler_params=_cp(("arbitrary",)),
        name="s5_out",
    )(z, s_re.reshape(2, nch, S5_BLOCKS * st), s_im.reshape(2, nch, S5_BLOCKS * st), p_re, p_im,
      bb_re, bb_im, c_re, c_im, d_skip.reshape(1, S5_WIDTH))


def _s5_glu_kernel(y_ref, w_ref, o_ref):
    y = y_ref[...]
    g = 0.5 * y * (1.0 + jnp.tanh(math.sqrt(2.0 / math.pi) * (y + 0.044715 * (y * y * y))))
    o_ref[...] = (g * _sigmoid(_dot(g.astype(BF16), w_ref[...].astype(BF16)))).astype(BF16)


def s5_glu(y, w_glu, l):
    n = y.shape[0]
    tm = _pick(n, (1056, 640, 256))
    return pl.pallas_call(
        _s5_glu_kernel,
        grid=(n // tm,),
        in_specs=[pl.BlockSpec((tm, S5_WIDTH), lambda i: (i, 0)),
                  pl.BlockSpec((None, S5_WIDTH, S5_WIDTH), lambda i: (l, 0, 0))],
        out_specs=pl.BlockSpec((tm, S5_WIDTH), lambda i: (i, 0)),
        out_shape=jax.ShapeDtypeStruct((n, S5_WIDTH), BF16),
        compiler_params=_cp(("arbitrary",)),
        name="s5_glu",
    )(y, w_glu)


def _gla_kernel(qf_ref, kf_ref, vf_ref, lrf_ref, qb_ref, kb_ref, vb_ref, lrb_ref, w2_ref, b_ref, of_ref, ob_ref,
                st_scr):
    @pl.when(pl.program_id(0) == 0)
    def _():
        st_scr[...] = jnp.zeros_like(st_scr)

    _gla_chunk(qf_ref, kf_ref, vf_ref, lrf_ref, w2_ref.at[0], b_ref.at[0], of_ref, st_scr.at[0], False)
    _gla_chunk(qb_ref, kb_ref, vb_ref, lrb_ref, w2_ref.at[1], b_ref.at[1], ob_ref, st_scr.at[1], True)


def _gla_chunk(q_ref, k_ref, v_ref, lr_ref, w2_ref, b_ref, o_ref, st_scr, rev):
    c = GLA_CHUNK
    x = _dot(lr_ref[...].astype(BF16), w2_ref[...].astype(BF16)) + b_ref[...]
    la = (jnp.minimum(x, 0.0) - jnp.log1p(jnp.exp(-jnp.abs(x)))) * (1.0 / GLA_TAU)
    ri = lax.broadcasted_iota(jnp.int32, (c, c), 0)
    ci = lax.broadcasted_iota(jnp.int32, (c, c), 1)
    keep = (ci >= ri) if rev else (ci <= ri)
    tri = jnp.where(keep, 1.0, 0.0).astype(BF16)
    hi = la.astype(BF16)
    r1 = la - hi.astype(F32)
    mid = r1.astype(BF16)
    lo = (r1 - mid.astype(F32)).astype(BF16)
    bc = _dot(tri, hi) + _dot(tri, mid) + _dot(tri, lo)
    last, ref_row = (0, c // 2) if rev else (c - 1, c // 2 - 1)
    btot = bc[last:last + 1]
    bmid = bc[ref_row:ref_row + 1]
    qs = q_ref[...].astype(F32) * (GLA_DK ** -0.5)
    kk = k_ref[...].astype(F32)
    q_mid = (qs * jnp.exp(bc - bmid)).astype(BF16)
    k_mid = (kk * jnp.exp(bmid - bc)).astype(BF16)
    q_in = (qs * jnp.exp(bc)).astype(BF16)
    k_out = (kk * jnp.exp(btot - bc)).astype(BF16)
    decay = jnp.exp(btot)
    for h in range(GLA_HEADS):
        ks = slice(h * GLA_DK, (h + 1) * GLA_DK)
        vs = slice(h * GLA_DV, (h + 1) * GLA_DV)
        vb = v_ref[:, vs].astype(BF16)
        sc = jnp.where(keep, _dot_nt(q_mid[:, ks], k_mid[:, ks]), 0.0).astype(BF16)
        st = st_scr[h]
        o_ref[:, vs] = _dot(sc, vb) + _dot_nt(q_in[:, ks], st.astype(BF16))
        st_scr[h] = decay[:, ks] * st + _dot_tn(vb, k_out[:, ks])


def gla_mix(z, lr, w2pad, bias, ctx_len):
    n = z.shape[0]
    c = GLA_CHUNK
    nch, ncc = n // c, ctx_len // c
    fwd = lambda i: i
    bwd = lambda i: jnp.where(i < ncc, ncc - 1 - i, nch + ncc - 1 - i)

    def chunk_specs(row):
        return [pl.BlockSpec((c, GLA_KW), lambda i: (row(i), OFF_GQ // GLA_KW)),
                pl.BlockSpec((c, GLA_KW), lambda i: (row(i), OFF_GK // GLA_KW)),
                pl.BlockSpec((c, GLA_VW), lambda i: (row(i), OFF_GV // GLA_VW)),
                pl.BlockSpec((c, LANES), lambda i: (row(i), 0))]

    o_shape = jax.ShapeDtypeStruct((n, GLA_VW), F32)
    return pl.pallas_call(
        _gla_kernel,
        grid=(nch,),
        in_specs=chunk_specs(fwd) + chunk_specs(bwd) + [
            pl.BlockSpec((2, LANES, GLA_KW), lambda i: (0, 0, 0)),
            pl.BlockSpec((2, 1, GLA_KW), lambda i: (0, 0, 0))],
        out_specs=[pl.BlockSpec((c, GLA_VW), lambda i: (fwd(i), 0)),
                   pl.BlockSpec((c, GLA_VW), lambda i: (bwd(i), 0))],
        out_shape=[o_shape, o_shape],
        scratch_shapes=[pltpu.VMEM((2, GLA_HEADS, GLA_DV, GLA_DK), F32)],
        compiler_params=_cp(("arbitrary",)),
        name="gla_mix",
    )(z, z, z, lr, z, z, z, lr, w2pad, bias)


def _gla_post_kernel(of_ref, ob_ref, r_ref, g_ref, y_ref):
    for h in range(GLA_HEADS):
        vs = slice(h * GLA_DV, (h + 1) * GLA_DV)
        o = of_ref[:, vs] + ob_ref[:, vs]
        rn = lax.rsqrt(jnp.mean(o * o, axis=-1, keepdims=True) + EPS)
        r = r_ref[:, vs].astype(F32)
        y_ref[:, vs] = ((o * rn) * g_ref[0, :, vs] * (r * _sigmoid(r))).astype(BF16)


def gla_post(o_f, o_b, z, norm_g, l):
    n = z.shape[0]
    tm = _pick(n, (1056, 640, 256))
    return pl.pallas_call(
        _gla_post_kernel,
        grid=(n // tm,),
        in_specs=[pl.BlockSpec((tm, GLA_VW), lambda i: (i, 0)),
                  pl.BlockSpec((tm, GLA_VW), lambda i: (i, 0)),
                  pl.BlockSpec((tm, GLA_VW), lambda i: (i, OFF_GR // GLA_VW)),
                  pl.BlockSpec((1, 1, GLA_VW), lambda i: (l, 0, 0))],
        out_specs=pl.BlockSpec((tm, GLA_VW), lambda i: (i, 0)),
        out_shape=jax.ShapeDtypeStruct((n, GLA_VW), BF16),
        compiler_params=_cp(("arbitrary",)),
        name="gla_post",
    )(o_f, o_b, z, norm_g.reshape(DEPTH, 1, GLA_VW))


def _att_prep_kernel(q_ref, k_ref, v_ref, cos_ref, sin_ref, qo_ref, ko_ref, vo_ref):
    cos = cos_ref[...]
    sin = sin_ref[...]
    lane = lax.broadcasted_iota(jnp.int32, (1, LANES), 1)
    first_half = (lane % ATT_HEAD_DIM) < (ATT_HEAD_DIM // 2)
    low_head = lane < ATT_HEAD_DIM

    def rope(xc):
        rot = jnp.where(first_half, pltpu.roll(xc, LANES - ATT_HEAD_DIM // 2, 1), pltpu.roll(xc, ATT_HEAD_DIM // 2, 1))
        return xc * cos + rot * sin

    def dup(xc, parity):
        sw = pltpu.roll(xc, ATT_HEAD_DIM, 1)
        return jnp.where(low_head, xc, sw) if parity == 0 else jnp.where(low_head, sw, xc)

    for cidx in range(ATT_QW // LANES):
        cs = slice(cidx * LANES, (cidx + 1) * LANES)
        qo_ref[:, cs] = (rope(q_ref[:, cs].astype(F32)) * (ATT_HEAD_DIM ** -0.5)).astype(BF16)
    for kh in range(ATT_KV_HEADS):
        ps = slice((kh // 2) * LANES, (kh // 2 + 1) * LANES)
        os_ = slice(kh * LANES, (kh + 1) * LANES)
        ko_ref[:, os_] = dup(rope(k_ref[:, ps].astype(F32)), kh % 2).astype(BF16)
        vo_ref[:, os_] = dup(v_ref[:, ps].astype(F32), kh % 2).astype(BF16)


def att_prep(z, cos_t, sin_t):
    n = z.shape[0]
    tm = 256
    kw = ATT_KV_HEADS * LANES
    return pl.pallas_call(
        _att_prep_kernel,
        grid=(n // tm,),
        in_specs=[pl.BlockSpec((tm, ATT_QW), lambda i: (i, OFF_AQ // ATT_QW)),
                  pl.BlockSpec((tm, ATT_KVW), lambda i: (i, OFF_AK // ATT_KVW)),
                  pl.BlockSpec((tm, ATT_KVW), lambda i: (i, OFF_AV // ATT_KVW)),
                  pl.BlockSpec((tm, LANES), lambda i: (i, 0)),
                  pl.BlockSpec((tm, LANES), lambda i: (i, 0))],
        out_specs=[pl.BlockSpec((tm, ATT_QW), lambda i: (i, 0)),
                   pl.BlockSpec((tm, kw), lambda i: (i, 0)),
                   pl.BlockSpec((tm, kw), lambda i: (i, 0))],
        out_shape=[jax.ShapeDtypeStruct((n, ATT_QW), BF16),
                   jax.ShapeDtypeStruct((n, kw), BF16),
                   jax.ShapeDtypeStruct((n, kw), BF16)],
        compiler_params=_cp(("arbitrary",)),
        name="att_prep",
    )(z, z, z, cos_t, sin_t)


def _att_heads(q_ref, k_all, v_all, bias, sink_ref, o_ref):
    lane = lax.broadcasted_iota(jnp.int32, (1, LANES), 1)
    low_head = lane < ATT_HEAD_DIM
    grp = ATT_HEADS // ATT_KV_HEADS
    blk = q_ref.shape[0]
    for kh in range(ATT_KV_HEADS):
        k2 = k_all[:, kh * LANES:(kh + 1) * LANES]
        v2 = v_all[:, kh * LANES:(kh + 1) * LANES]
        qs, sinks = [], []
        for j in range(grp):
            h = kh * grp + j
            qp = q_ref[:, (h // 2) * LANES:(h // 2 + 1) * LANES]
            qs.append(jnp.where(low_head if h % 2 == 0 else jnp.logical_not(low_head), qp, jnp.zeros_like(qp)))
            sinks.append(jnp.full((blk, 1), sink_ref[h], F32))
        s = _dot_nt(jnp.concatenate(qs, axis=0), k2)
        if bias is not None:
            nband = bias.shape[1]
            s = jnp.concatenate([s[:, :nband] + jnp.concatenate([bias] * grp, axis=0), s[:, nband:]], axis=1)
        sk = jnp.concatenate(sinks, axis=0)
        m = jnp.maximum(jnp.max(s, axis=-1, keepdims=True), sk)
        p = jnp.exp(s - m)
        den = jnp.sum(p, axis=-1, keepdims=True) + jnp.exp(sk - m)
        o = _dot(p.astype(BF16), v2) * (1.0 / den)
        for pr in range(grp // 2):
            pair = kh * (grp // 2) + pr
            even, odd = o[(2 * pr) * blk:(2 * pr + 1) * blk], o[(2 * pr + 1) * blk:(2 * pr + 2) * blk]
            o_ref[:, pair * LANES:(pair + 1) * LANES] = jnp.where(low_head, even, odd).astype(BF16)


def _att_kernel(sink_ref, q_ref, kp_ref, kc_ref, kn_ref, kx_ref, vp_ref, vc_ref, vn_ref, vx_ref, o_ref, *, seq):
    i = pl.program_id(0)
    blk = ATT_BLOCK
    k_all = jnp.concatenate([kp_ref[...], kc_ref[...], kn_ref[...], kx_ref[...]], axis=0)
    v_all = jnp.concatenate([vp_ref[...], vc_ref[...], vn_ref[...], vx_ref[...]], axis=0)
    nctx = kx_ref.shape[0]
    qpos = i * blk + lax.broadcasted_iota(jnp.int32, (blk, 1), 0)
    kpos = (i - 1) * blk + lax.broadcasted_iota(jnp.int32, (1, 3 * blk), 1)
    valid = (jnp.abs(kpos - qpos) <= WINDOW) & (kpos >= 0) & (kpos < seq)
    _att_heads(q_ref, k_all, v_all, jnp.where(valid, 0.0, -jnp.inf), sink_ref, o_ref)


def _att_ctx_kernel(sink_ref, q_ref, kx_ref, vx_ref, o_ref):
    _att_heads(q_ref, kx_ref[...], vx_ref[...], None, sink_ref, o_ref)


def attention(qr, k2, v2, sink, ctx_len):
    n = qr.shape[0]
    blk = ATT_BLOCK
    seq = n - ctx_len
    nb, cb = seq // blk, ctx_len // blk
    kw = ATT_KV_HEADS * LANES
    prev = lambda i, s: (jnp.maximum(i - 1, 0) + cb, 0)
    cur = lambda i, s: (i + cb, 0)
    nxt = lambda i, s: (jnp.minimum(i + 1, nb - 1) + cb, 0)
    cx = lambda i, s: (0, 0)
    band = [pl.BlockSpec((blk, kw), prev), pl.BlockSpec((blk, kw), cur), pl.BlockSpec((blk, kw), nxt),
            pl.BlockSpec((ctx_len, kw), cx)]
    y_lat = pl.pallas_call(
        functools.partial(_att_kernel, seq=seq),
        grid_spec=pltpu.PrefetchScalarGridSpec(
            num_scalar_prefetch=1, grid=(nb,),
            in_specs=[pl.BlockSpec((blk, ATT_QW), cur)] + band + band,
            out_specs=pl.BlockSpec((blk, ATT_QW), lambda i, s: (i, 0))),
        out_shape=jax.ShapeDtypeStruct((seq, ATT_QW), BF16),
        compiler_params=_cp(("arbitrary",)),
        name="att_window",
    )(sink, qr, k2, k2, k2, k2, v2, v2, v2, v2)
    y_ctx = pl.pallas_call(
        _att_ctx_kernel,
        grid_spec=pltpu.PrefetchScalarGridSpec(
            num_scalar_prefetch=1, grid=(cb,),
            in_specs=[pl.BlockSpec((blk, ATT_QW), lambda i, s: (i, 0)),
                      pl.BlockSpec((ctx_len, kw), cx), pl.BlockSpec((ctx_len, kw), cx)],
            out_specs=pl.BlockSpec((blk, ATT_QW), lambda i, s: (i, 0))),
        out_shape=jax.ShapeDtypeStruct((ctx_len, ATT_QW), BF16),
        compiler_params=_cp(("arbitrary",)),
        name="att_context",
    )(sink, qr, k2, v2)
    return jnp.concatenate([y_ctx, y_lat], axis=0)


def _merge_kernel(a1_ref, a2_ref, a3_ref, w1_ref, w2_ref, w3_ref, g1_ref, g2_ref, g3_ref, o_ref):
    acc = _sigmoid(g1_ref[...].astype(F32)) * _dot(a1_ref[...], w1_ref[...].astype(BF16))
    acc = acc + _sigmoid(g2_ref[...].astype(F32)) * _dot(a2_ref[...], w2_ref[...].astype(BF16))
    acc = acc + _sigmoid(g3_ref[...].astype(F32)) * _dot(a3_ref[...], w3_ref[...].astype(BF16))
    o_ref[...] = acc.astype(BF16)


def merge_branches(y_s5, y_gla, y_att, z, w_s5, w_gla, w_att, l):
    n = z.shape[0]
    tm = _pick(n, (1056, 640, 256))
    tn = 512
    a_spec = pl.BlockSpec((tm, S5_WIDTH), lambda j, i: (i, 0))
    w_spec = pl.BlockSpec((None, S5_WIDTH, tn), lambda j, i: (l, 0, j))

    def g_spec(b):
        return pl.BlockSpec((tm, tn), lambda j, i: (i, (OFF_GATE + b * D_MODEL) // tn + j))

    return pl.pallas_call(
        _merge_kernel,
        grid=(D_MODEL // tn, n // tm),
        in_specs=[a_spec, a_spec, a_spec, w_spec, w_spec, w_spec, g_spec(0), g_spec(1), g_spec(2)],
        out_specs=pl.BlockSpec((tm, tn), lambda j, i: (i, j)),
        out_shape=jax.ShapeDtypeStruct((n, D_MODEL), BF16),
        compiler_params=_cp(("arbitrary", "arbitrary")),
        name="merge",
    )(y_s5, y_gla, y_att, w_s5, w_gla, w_att, z, z, z)


def _out_kernel(m_ref, w_ref, x_ref, mod_ref, g_ref, rt_ref, x1_ref, h2_ref, lg_ref, *, ctx_len, tm):
    i = pl.program_id(0)
    is_ctx = (i * tm + lax.broadcasted_iota(jnp.int32, (tm, 1), 0)) < ctx_len
    x1 = x_ref[...] + _row_mods(mod_ref, 2, is_ctx) * _dot(m_ref[...], w_ref[...])
    x1_ref[...] = x1
    r = lax.rsqrt(jnp.mean(x1 * x1, axis=-1, keepdims=True) + EPS)
    h2 = ((x1 * r) * g_ref[0]) * (1.0 + _row_mods(mod_ref, 4, is_ctx)) + _row_mods(mod_ref, 3, is_ctx)
    h_hi = h2.astype(BF16)
    h2_ref[...] = h_hi
    h_lo = (h2 - h_hi.astype(F32)).astype(BF16)
    rt = rt_ref[...]
    r_hi = rt.astype(BF16)
    r_lo = (rt - r_hi.astype(F32)).astype(BF16)
    lg_ref[...] = _dot(h_hi, r_hi) + (_dot(h_lo, r_hi) + _dot(h_hi, r_lo))


def out_projection(m, w_out_bf, xs, mods, norm_g, router_pad, l, ctx_len):
    n = xs.shape[0]
    tm = _pick(n, (528, 640, 256))
    once = pl.Buffered(1)
    return pl.pallas_call(
        functools.partial(_out_kernel, ctx_len=ctx_len, tm=tm),
        grid=(n // tm,),
        in_specs=[pl.BlockSpec((tm, D_MODEL), lambda i: (i, 0)),
                  pl.BlockSpec((None, D_MODEL, D_MODEL), lambda i: (l, 0, 0), pipeline_mode=once),
                  pl.BlockSpec((tm, D_MODEL), lambda i: (i, 0)),
                  pl.BlockSpec((None, 8, N_MOD * D_MODEL), lambda i: (l, 0, 0)),
                  pl.BlockSpec((1, 1, D_MODEL), lambda i: (l, 0, 0)),
                  pl.BlockSpec((None, D_MODEL, LANES), lambda i: (l, 0, 0), pipeline_mode=once)],
        out_specs=[pl.BlockSpec((tm, D_MODEL), lambda i: (i, 0)),
                   pl.BlockSpec((tm, D_MODEL), lambda i: (i, 0)),
                   pl.BlockSpec((tm, LANES), lambda i: (i, 0))],
        out_shape=[jax.ShapeDtypeStruct((n, D_MODEL), F32),
                   jax.ShapeDtypeStruct((n, D_MODEL), BF16),
                   jax.ShapeDtypeStruct((n, LANES), F32)],
        compiler_params=_cp(("arbitrary",)),
        name="out_projection",
    )(m, w_out_bf, xs, mods, norm_g.reshape(DEPTH, 1, D_MODEL), router_pad)


def _route_kernel(lg_ref, aff_ref, slot_ref, slot_t_ref, s0_ref, cnt_ref, aff_t_scr, *, ctx_len, n):
    tt = TOK_TILE
    lane = lax.broadcasted_iota(jnp.int32, (1, LANES), 1)
    is_expert = lane < N_EXPERTS

    def softmax_tile(b, _):
        r0 = pl.multiple_of(b * tt, tt)
        x = jnp.where(is_expert, lg_ref[pl.ds(r0, tt), :], -jnp.inf)
        e = jnp.exp(x - jnp.max(x, axis=-1, keepdims=True))
        aff = e / jnp.sum(e, axis=-1, keepdims=True)
        aff_ref[pl.ds(r0, tt), :] = aff
        aff_t_scr[:, pl.ds(r0, tt)] = jnp.transpose(aff)[0:N_EXPERTS, :]
        return 0

    lax.fori_loop(0, n // tt, softmax_tile, 0)

    ri = lax.broadcasted_iota(jnp.int32, (tt, tt), 0)
    ci = lax.broadcasted_iota(jnp.int32, (tt, tt), 1)
    before = jnp.where(ci < ri, 1.0, 0.0).astype(BF16)
    er = lax.broadcasted_iota(jnp.int32, (N_EXPERTS, LANES), 0)
    el = lax.broadcasted_iota(jnp.int32, (N_EXPERTS, LANES), 1)

    def as_row(col):
        return jnp.sum(jnp.where(er == el, jnp.broadcast_to(col, (N_EXPERTS, LANES)), 0), axis=0, keepdims=True)

    def bits_of(b):
        r0 = pl.multiple_of(b * tt, tt)
        return pltpu.bitcast(aff_ref[pl.ds(r0, tt), :], jnp.int32)

    def route_set(b0, b1, slot_base):
        cap = float(EC_CAPACITY * (b1 - b0) * tt // N_EXPERTS)
        def count(pred):
            hit = pred(pltpu.bitcast(aff_t_scr[:, b0 * tt:b1 * tt], jnp.int32))
            return jnp.sum(jnp.where(hit, 1.0, 0.0), axis=-1, keepdims=True)

        def bisect(_, lh):
            lo, hi = lh
            mid = lo + lax.shift_right_logical(hi - lo + 1, 1)
            ok = count(lambda v: v >= mid) >= cap
            return jnp.where(ok, mid, lo), jnp.where(ok, hi, mid - 1)

        lo0 = jnp.zeros((N_EXPERTS, 1), jnp.int32)
        hi0 = jnp.full((N_EXPERTS, 1), 0x7F800000, jnp.int32)
        thr_col, _ = lax.fori_loop(0, 31, bisect, (lo0, hi0))
        need = as_row(cap - count(lambda v: v > thr_col))
        thr = as_row(thr_col)

        def assign(b, carry):
            tie_seen, taken = carry
            r0 = pl.multiple_of(b * tt, tt)
            v = bits_of(b)
            tie = jnp.where(v == thr, 1.0, 0.0)
            tie_rank = _dot(before, tie.astype(BF16)) + tie_seen
            sel = jnp.where((v > thr) | ((v == thr) & (tie_rank < need)), 1.0, 0.0)
            rank = _dot(before, sel.astype(BF16)) + taken
            slot = jnp.where(sel > 0.0, rank + slot_base, -1.0)
            slot_ref[pl.ds(r0, tt), :] = slot.astype(jnp.int32)
            slot_t_ref[:, pl.ds(r0, tt)] = jnp.transpose(slot)[0:N_EXPERTS, :].astype(jnp.int32)
            n_sel = jnp.sum(sel, axis=0, keepdims=True)
            s0_ref[b] = (taken + slot_base).astype(jnp.int32)
            cnt_ref[b] = n_sel.astype(jnp.int32)
            return tie_seen + jnp.sum(tie, axis=0, keepdims=True), taken + n_sel

        zero = jnp.zeros((1, LANES), F32)
        lax.fori_loop(b0, b1, assign, (zero, zero))
        return cap

    cb = ctx_len // tt
    cap_ctx = route_set(0, cb, 0.0)
    route_set(cb, n // tt, cap_ctx)


def route(logits, ctx_len):
    n = logits.shape[0]
    nt = n // TOK_TILE
    full = lambda shape: pl.BlockSpec(shape, lambda i: tuple(0 for _ in shape))
    return pl.pallas_call(
        functools.partial(_route_kernel, ctx_len=ctx_len, n=n),
        grid=(1,),
        in_specs=[full((n, LANES))],
        out_specs=[full((n, LANES)), full((n, LANES)), full((N_EXPERTS, n)), full((nt, 1, LANES)),
                   full((nt, 1, LANES))],
        out_shape=[jax.ShapeDtypeStruct((n, LANES), F32),
                   jax.ShapeDtypeStruct((n, LANES), jnp.int32),
                   jax.ShapeDtypeStruct((N_EXPERTS, n), jnp.int32),
                   jax.ShapeDtypeStruct((nt, 1, LANES), jnp.int32),
                   jax.ShapeDtypeStruct((nt, 1, LANES), jnp.int32)],
        scratch_shapes=[pltpu.VMEM((N_EXPERTS, n), F32)],
        compiler_params=_cp(("arbitrary",)),
        name="route",
    )(logits)


def _window_plan(s0, cnt, rows, align_bits=3):
    a0 = lax.shift_left(lax.shift_right_logical(s0, align_bits), align_bits)
    nwin = jnp.where(cnt > 0, lax.shift_right_logical(s0 - a0 + cnt + MOE_WIN - 1, MOE_WIN.bit_length() - 1), 0)
    return a0, nwin, rows - MOE_WIN


GATHER_GROUP = 4
BF16_ROWS_LOG2 = 4


def _gather_kernel(s0_ref, cnt_ref, h_ref, slot_ref, xs_ref, *, rows):
    g = pl.program_id(0)
    t = pl.program_id(1)

    @pl.when(t == 0)
    def _():
        xs_ref[...] = jnp.zeros_like(xs_ref)

    def window(i, j, a0, last):
        start = a0 + j * MOE_WIN
        base = pl.multiple_of(jnp.minimum(start, last), 1 << BF16_ROWS_LOG2)
        want = base + lax.broadcasted_iota(jnp.int32, (MOE_WIN, 1), 0)
        want = jnp.where(want >= start, want, -2)
        return base, jnp.where(slot_ref[i] == want, 1.0, 0.0).astype(BF16)

    def add_rows(i, base, new):
        prev = xs_ref[i, pl.ds(base, MOE_WIN), :].astype(F32)
        xs_ref[i, pl.ds(base, MOE_WIN), :] = (prev + new).astype(BF16)

    plans = [_window_plan(s0_ref[t, g * GATHER_GROUP + i], cnt_ref[t, g * GATHER_GROUP + i], rows, BF16_ROWS_LOG2)
             for i in range(GATHER_GROUP)]
    firsts = [window(i, 0, plans[i][0], plans[i][2]) for i in range(GATHER_GROUP)]
    packed = _dot(jnp.concatenate([oh for _, oh in firsts], axis=0), h_ref[...])
    for i in range(GATHER_GROUP):
        add_rows(i, firsts[i][0], packed[i * MOE_WIN:(i + 1) * MOE_WIN])

        def body(j, _, i=i):
            base, onehot = window(i, j, plans[i][0], plans[i][2])
            add_rows(i, base, _dot(onehot, h_ref[...]))
            return 0

        lax.fori_loop(1, plans[i][1], body, 0)


def moe_gather(h2, slot_t, s0, cnt, rows):
    n = h2.shape[0]
    nt = n // TOK_TILE
    gg = GATHER_GROUP
    return pl.pallas_call(
        functools.partial(_gather_kernel, rows=rows),
        grid_spec=pltpu.PrefetchScalarGridSpec(
            num_scalar_prefetch=2, grid=(N_EXPERTS // gg, nt),
            in_specs=[pl.BlockSpec((TOK_TILE, D_MODEL), lambda g, t, a, b: (t, 0)),
                      pl.BlockSpec((gg, 1, TOK_TILE), lambda g, t, a, b: (g, 0, t))],
            out_specs=pl.BlockSpec((gg, rows, D_MODEL), lambda g, t, a, b: (g, 0, 0))),
        out_shape=jax.ShapeDtypeStruct((N_EXPERTS, rows, D_MODEL), BF16),
        compiler_params=_cp(("arbitrary", "arbitrary")),
        name="moe_gather",
    )(s0, cnt, h2, slot_t.reshape(N_EXPERTS, 1, n))


def _ffn_kernel(xs_ref, wg_ref, wu_ref, wd_ref, y_ref, act_scr):
    f = pl.program_id(1)
    nf = EXPERT_FF // FF_TILE

    @pl.when(f < nf)
    def _():
        x = xs_ref[0]
        a = _dot(x, wg_ref[0].astype(BF16))
        u = _dot(x, wu_ref[0].astype(BF16))
        act_scr[:, pl.ds(pl.multiple_of(f * FF_TILE, FF_TILE), FF_TILE)] = ((a * _sigmoid(a)) * u).astype(BF16)

    @pl.when(f >= nf)
    def _():
        y_ref[0] = _dot(act_scr[...], wd_ref[0].astype(BF16)).astype(BF16)


def moe_ffn(xs, w_gate, w_up, w_down, l):
    rows = xs.shape[1]
    nf = EXPERT_FF // FF_TILE
    up_idx = lambda e, f: (l, e, 0, jnp.minimum(f, nf - 1))
    return pl.pallas_call(
        _ffn_kernel,
        grid=(N_EXPERTS, nf + D_MODEL // FF_TILE),
        in_specs=[pl.BlockSpec((1, rows, D_MODEL), lambda e, f: (e, 0, 0)),
                  pl.BlockSpec((None, 1, D_MODEL, FF_TILE), up_idx),
                  pl.BlockSpec((None, 1, D_MODEL, FF_TILE), up_idx),
                  pl.BlockSpec((None, 1, EXPERT_FF, FF_TILE), lambda e, f: (l, e, 0, jnp.maximum(f - nf, 0)))],
        out_specs=pl.BlockSpec((1, rows, FF_TILE), lambda e, f: (e, 0, jnp.maximum(f - nf, 0))),
        out_shape=jax.ShapeDtypeStruct((N_EXPERTS, rows, D_MODEL), BF16),
        scratch_shapes=[pltpu.VMEM((rows, EXPERT_FF), BF16)],
        compiler_params=_cp(("arbitrary", "arbitrary")),
        name="moe_ffn",
    )(xs, w_gate, w_up, w_down)


MORE_WINS = TOK_TILE // MOE_WIN


def _combine_kernel(s0_ref, cnt_ref, x_ref, slot_ref, aff_ref, mod_ref, g_ref, y_hbm, o_ref, buf0, bufx, sem0, semx,
                    *, rows, ctx_len, final):
    t = pl.program_id(0)
    nt = pl.num_programs(0)
    tt = TOK_TILE

    def first_copy(tile, e, par):
        a0, _, last = _window_plan(s0_ref[tile, e], cnt_ref[tile, e], rows)
        base = pl.multiple_of(jnp.minimum(a0, last), 8)
        cp = pltpu.make_async_copy(y_hbm.at[e, pl.ds(base, MOE_WIN)],
                                   buf0.at[par, pl.ds(e * MOE_WIN, MOE_WIN)], sem0.at[par, e])
        return base, a0, cp

    def more_copy(e, j, a0, last):
        base = pl.multiple_of(jnp.minimum(a0 + j * MOE_WIN, last), 8)
        k = e * MORE_WINS + j - 1
        return base, pltpu.make_async_copy(y_hbm.at[e, pl.ds(base, MOE_WIN)], bufx.at[k], semx.at[k])

    @pl.when(t == 0)
    def _():
        for e in range(N_EXPERTS):
            first_copy(0, e, 0)[2].start()

    @pl.when(t + 1 < nt)
    def _():
        for e in range(N_EXPERTS):
            first_copy(t + 1, e, (t + 1) % 2)[2].start()

    for e in range(N_EXPERTS):
        a0, nwin, last = _window_plan(s0_ref[t, e], cnt_ref[t, e], rows)

        def start(j, _, e=e, a0=a0, last=last):
            more_copy(e, j, a0, last)[1].start()
            return 0

        lax.fori_loop(1, nwin, start, 0)

    par = t % 2
    lane = lax.broadcasted_iota(jnp.int32, (1, LANES), 1)
    left = lane < MOE_WIN
    offs = jnp.where(left, lane, lane - MOE_WIN)
    cols_hi, cols_lo = [], []
    for c in range(N_EXPERTS * MOE_WIN // LANES):
        e0, e1 = 2 * c, 2 * c + 1
        b0, a00, cp0 = first_copy(t, e0, par)
        b1, a01, cp1 = first_copy(t, e1, par)
        cp0.wait()
        cp1.wait()
        want = jnp.where(left, b0, b1) + offs
        want = jnp.where(want >= jnp.where(left, a00, a01), want, -2)
        picked = jnp.where(left, slot_ref[:, e0:e0 + 1], slot_ref[:, e1:e1 + 1]) == want
        gate = jnp.where(picked, jnp.where(left, aff_ref[:, e0:e0 + 1], aff_ref[:, e1:e1 + 1]), 0.0)
        g_hi = gate.astype(BF16)
        cols_hi.append(g_hi)
        cols_lo.append((gate - g_hi.astype(F32)).astype(BF16))
    y0 = buf0[par]
    o_ref[...] = _dot(jnp.concatenate(cols_hi, axis=-1), y0) + _dot(jnp.concatenate(cols_lo, axis=-1), y0)

    for e in range(N_EXPERTS):
        a0, nwin, last = _window_plan(s0_ref[t, e], cnt_ref[t, e], rows)

        def take(j, _, e=e, a0=a0, last=last):
            base, cp = more_copy(e, j, a0, last)
            cp.wait()
            want = base + lax.broadcasted_iota(jnp.int32, (1, MOE_WIN), 1)
            want = jnp.where(want >= a0 + j * MOE_WIN, want, -2)
            onehot = jnp.where(slot_ref[:, e:e + 1] == want, 1.0, 0.0).astype(BF16)
            o_ref[...] += aff_ref[:, e:e + 1] * _dot(onehot, bufx[e * MORE_WINS + j - 1])
            return 0

        lax.fori_loop(1, nwin, take, 0)

    is_ctx = (t * tt + lax.broadcasted_iota(jnp.int32, (tt, 1), 0)) < ctx_len
    x2 = x_ref[...] + _row_mods(mod_ref, 5, is_ctx) * o_ref[...]
    if final:
        r = lax.rsqrt(jnp.mean(x2 * x2, axis=-1, keepdims=True) + EPS)
        x2 = (x2 * r) * g_ref[...]
    o_ref[...] = x2


def moe_combine(x1, slot, aff, s0, cnt, mods, y, final_g, l, ctx_len, final):
    n = x1.shape[0]
    rows = y.shape[1]
    tt = TOK_TILE
    nbuf = N_EXPERTS * MORE_WINS
    return pl.pallas_call(
        functools.partial(_combine_kernel, rows=rows, ctx_len=ctx_len, final=final),
        grid_spec=pltpu.PrefetchScalarGridSpec(
            num_scalar_prefetch=2, grid=(n // tt,),
            in_specs=[pl.BlockSpec((tt, D_MODEL), lambda t, a, b: (t, 0)),
                      pl.BlockSpec((tt, LANES), lambda t, a, b: (t, 0)),
                      pl.BlockSpec((tt, LANES), lambda t, a, b: (t, 0)),
                      pl.BlockSpec((None, 8, N_MOD * D_MODEL), lambda t, a, b: (l, 0, 0)),
                      pl.BlockSpec((1, D_MODEL), lambda t, a, b: (0, 0)),
                      pl.BlockSpec(memory_space=pl.ANY)],
            out_specs=pl.BlockSpec((tt, D_MODEL), lambda t, a, b: (t, 0)),
            scratch_shapes=[pltpu.VMEM((2, N_EXPERTS * MOE_WIN, D_MODEL), BF16),
                            pltpu.VMEM((nbuf, MOE_WIN, D_MODEL), BF16),
                            pltpu.SemaphoreType.DMA((2, N_EXPERTS)),
                            pltpu.SemaphoreType.DMA((nbuf,))]),
        out_shape=jax.ShapeDtypeStruct((n, D_MODEL), F32),
        compiler_params=_cp(("arbitrary",)),
        name="moe_combine",
    )(s0, cnt, x1, slot, aff, mods, final_g.reshape(1, D_MODEL), y)


def rope_tables(seq, ctx_len):
    pos = jnp.arange(seq)
    row = (pos // GRID_W).astype(F32)
    col = (pos % GRID_W).astype(F32)
    n_freq = ATT_HEAD_DIM // 4
    inv_freq = ROPE_BASE ** (-jnp.arange(n_freq, dtype=F32) / n_freq)
    ang = jnp.concatenate([row[:, None] * inv_freq, col[:, None] * inv_freq], axis=-1)
    cos, sin = jnp.cos(ang), jnp.sin(ang)
    cos_t = jnp.concatenate([jnp.ones((ctx_len, LANES), F32), jnp.tile(cos, (1, 4))], axis=0)
    sin_t = jnp.concatenate([jnp.zeros((ctx_len, LANES), F32), jnp.tile(jnp.concatenate([-sin, sin], -1), (1, 2))],
                            axis=0)
    return cos_t, sin_t


def kernel(x, c, ctx, c_ctx, ada_w, ada_b, norm1_g, norm2_g, w_in, s5_lam_re, s5_lam_im, s5_log_dt,
           s5_b_re, s5_b_im, s5_c_re, s5_c_im, s5_d, s5_w_glu, gla_w1, gla_w2, gla_b, gla_norm_g,
           attn_sink, w_branch_s5, w_branch_gla, w_branch_attn, w_out, moe_router, moe_w_gate,
           moe_w_up, moe_w_down, final_g):
    seq, ctx_len = x.shape[1], ctx.shape[1]
    n = seq + ctx_len
    assert x.shape[0] == 1 and ctx_len % TOK_TILE == 0 and seq % TOK_TILE == 0
    xs = jnp.concatenate([ctx[0], x[0]], axis=0)
    c8 = jnp.zeros((8, D_MODEL), F32).at[0].set(c[0]).at[1].set(c_ctx)
    mods = ada_mods(c8, ada_w, ada_b)
    cos_t, sin_t = rope_tables(seq, ctx_len)
    w_out_bf = w_out.astype(BF16)
    w_in_bf = w_in.astype(BF16)
    router_pad = jnp.pad(moe_router, ((0, 0), (0, 0), (0, LANES - N_EXPERTS)))
    rows = EC_CAPACITY * n // N_EXPERTS
    for l in range(DEPTH):
        w1cat = jnp.pad(jnp.concatenate([gla_w1[l, 0], gla_w1[l, 1]], axis=-1),
                        ((0, 0), (0, LANES - 2 * GLA_RANK)))
        w2pad = jnp.zeros((2, LANES, GLA_KW), F32)
        w2pad = w2pad.at[0, 0:GLA_RANK].set(gla_w2[l, 0]).at[1, GLA_RANK:2 * GLA_RANK].set(gla_w2[l, 1])
        zu, z, lr = in_projection(xs, norm1_g, mods, w_in_bf, w1cat, l, ctx_len)
        s5p = s5_params(s5_lam_re[l], s5_lam_im[l], s5_log_dt[l], s5_b_re[l], s5_b_im[l])
        y_s5 = s5_glu(s5_mix(zu, s5p, s5_c_re[l], s5_c_im[l], s5_d[l], ctx_len), s5_w_glu, l)
        gla_bias = gla_b[l].reshape(2, 1, GLA_KW)
        o_f, o_b = gla_mix(z, lr, w2pad, gla_bias, ctx_len)
        y_gla = gla_post(o_f, o_b, z, gla_norm_g, l)
        qr, k2, v2 = att_prep(z, cos_t, sin_t)
        y_att = attention(qr, k2, v2, attn_sink[l], ctx_len)
        m = merge_branches(y_s5, y_gla, y_att, z, w_branch_s5, w_branch_gla, w_branch_attn, l)
        x1, h2, logits = out_projection(m, w_out_bf, xs, mods, norm2_g, router_pad, l, ctx_len)
        aff, slot, slot_t, s0, cnt = route(logits, ctx_len)
        s0, cnt = s0[:, 0, :N_EXPERTS], cnt[:, 0, :N_EXPERTS]
        xe = moe_gather(h2, slot_t, s0, cnt, rows)
        ye = moe_ffn(xe, moe_w_gate, moe_w_up, moe_w_down, l)
        xs = moe_combine(x1, slot, aff, s0, cnt, mods, ye, final_g, l, ctx_len, l == DEPTH - 1)
    return xs[ctx_len:][None]
```
